```python
import jax, jax.numpy as jnp
from jax import lax
import numpy as np

D_MODEL = 1024
BATCH = 2
SEQ = 8192
DEPTH = 2
DEC_BATCH = 32
DEC_SEQ = 4
PAST_LEN = 16384
PAGE_SIZE = 128

N_HEADS = 16
HEAD_DIM = D_MODEL // N_HEADS
ROT_DIM = HEAD_DIM // 4
ROPE_THETA = 500000.0
MOBA_BLOCK = 256
MOBA_TOPK = 3
Q_BLOCK = 64
CONV_WIDTH = 31
N_GROUPS = 4
EXPERTS_PER_GROUP = 8
N_EXPERTS = N_GROUPS * EXPERTS_PER_GROUP
D_EXPERT = 256
EXPERT_TOPK = 2
NORM_EPS = 1e-6
N_ATTN_LAYERS = (DEPTH + 1) // 2
N_CONV_LAYERS = DEPTH // 2
F32 = jnp.float32

kernel_name = 'moba_conformer_hmoe_decode_step'


def rms_norm(x, g):
    xf = x.astype(F32)
    y = xf * lax.rsqrt(jnp.mean(xf * xf, axis=-1, keepdims=True) + NORM_EPS)
    return (y * g.astype(F32)).astype(x.dtype)


def layer_norm(x, g, b):
    xf = x.astype(F32)
    mu = jnp.mean(xf, axis=-1, keepdims=True)
    var = jnp.mean(jnp.square(xf - mu), axis=-1, keepdims=True)
    return ((xf - mu) * lax.rsqrt(var + NORM_EPS) * g.astype(F32) + b.astype(F32)).astype(x.dtype)


def partial_rope(x, pos):
    half = ROT_DIM // 2
    inv_freq = jnp.power(jnp.asarray(ROPE_THETA, F32), -jnp.arange(0, ROT_DIM, 2, dtype=F32) / ROT_DIM)
    ang = pos.astype(F32)[:, None] * inv_freq[None, :]
    cos = jnp.cos(ang)[None, :, None, :]
    sin = jnp.sin(ang)[None, :, None, :]
    xr = x[..., :ROT_DIM].astype(F32)
    x1, x2 = xr[..., :half], xr[..., half:]
    rot = jnp.concatenate([x1 * cos - x2 * sin, x2 * cos + x1 * sin], axis=-1).astype(x.dtype)
    return jnp.concatenate([rot, x[..., ROT_DIM:]], axis=-1)


def qkv_projection(xn, w_qkv, pos):
    b, t, _ = xn.shape
    qkv = (xn @ w_qkv).reshape(b, t, 3, N_HEADS, HEAD_DIM)
    return partial_rope(qkv[:, :, 0], pos), partial_rope(qkv[:, :, 1], pos), qkv[:, :, 2]


def moba_prompt(q, k, v):
    b, s, h, dh = q.shape
    nb = -(-s // MOBA_BLOCK)
    pad = ((0, 0), (0, nb * MOBA_BLOCK - s), (0, 0), (0, 0))
    kb = jnp.pad(k, pad).reshape(b, nb, MOBA_BLOCK, h, dh).transpose(0, 3, 1, 2, 4)
    vb = jnp.pad(v, pad).reshape(b, nb, MOBA_BLOCK, h, dh).transpose(0, 3, 1, 2, 4)
    q_pos = jnp.arange(s, dtype=jnp.int32)
    q_blk = q_pos // MOBA_BLOCK
    own = jnp.broadcast_to(q_blk[None, :, None, None], (b, s, h, 1))
    n_sel = min(MOBA_TOPK, nb - 1)
    if n_sel > 0:
        k_mean = jnp.mean(kb, axis=3, dtype=F32)
        gate = jnp.einsum('bshd,bhnd->bshn', q.astype(F32), k_mean)
        fully_past = jnp.arange(nb, dtype=jnp.int32)[None, :] < q_blk[:, None]
        gate = jnp.where(fully_past[None, :, None, :], gate, -jnp.inf)
        _, sel = lax.top_k(gate, n_sel)
        blk_idx = jnp.concatenate([sel, own], axis=-1)
        slot_ok = jnp.concatenate([sel < q_blk[None, :, None, None], jnp.ones(own.shape, bool)], axis=-1)
    else:
        blk_idx = own
        slot_ok = jnp.ones(own.shape, bool)
    n_slots = blk_idx.shape[-1]
    nq = s // Q_BLOCK
    xs = (jnp.repeat(jnp.arange(b, dtype=jnp.int32), nq),
          q.reshape(b * nq, Q_BLOCK, h, dh),
          blk_idx.reshape(b * nq, Q_BLOCK, h, n_slots),
          slot_ok.reshape(b * nq, Q_BLOCK, h, n_slots),
          jnp.tile(q_pos.reshape(nq, Q_BLOCK), (b, 1)))
    head = jnp.arange(h, dtype=jnp.int32)[None, :, None]
    offs = jnp.arange(MOBA_BLOCK, dtype=jnp.int32)
    scale = HEAD_DIM ** -0.5

    def query_block(args):
        bi, qb, idx, ok, pos = args
        kg = kb[bi, head, idx]
        vg = vb[bi, head, idx]
        k_pos = idx[..., None] * MOBA_BLOCK + offs
        mask = ok[..., None] & (k_pos <= pos[:, None, None, None])
        sc = jnp.einsum('qhd,qhnkd->qhnk', qb, kg, preferred_element_type=F32) * scale
        sc = jnp.where(mask, sc, -jnp.inf).reshape(Q_BLOCK, h, n_slots * MOBA_BLOCK)
        p = jax.nn.softmax(sc, axis=-1).reshape(Q_BLOCK, h, n_slots, MOBA_BLOCK).astype(vg.dtype)
        return jnp.einsum('qhnk,qhnkd->qhd', p, vg)

    out = lax.map(query_block, xs)
    return out.reshape(b, s, h * dh)


def moba_sample(q, k_new, v_new, cache_k, cache_v, page_table, layer):
    db, t, h, dh = q.shape
    ppb = MOBA_BLOCK // PAGE_SIZE
    nb_past = PAST_LEN // MOBA_BLOCK
    n_sel = min(MOBA_TOPK, nb_past)
    scale = HEAD_DIM ** -0.5
    scores, values = [], []
    if n_sel > 0:
        past_pages = cache_k[layer, page_table[:, :nb_past * ppb]]
        k_mean = jnp.mean(past_pages.reshape(db, nb_past, ppb, h, PAGE_SIZE, dh), axis=(2, 4), dtype=F32)
        gate = jnp.einsum('bthd,bnhd->bthn', q.astype(F32), k_mean)
        _, sel = lax.top_k(gate, n_sel)
        logical = sel[..., None] * ppb + jnp.arange(ppb, dtype=jnp.int32)
        phys = page_table[jnp.arange(db, dtype=jnp.int32)[:, None, None, None, None], logical]
        head = jnp.arange(h, dtype=jnp.int32)[None, None, :, None, None]
        k_sel = cache_k[layer, phys, head].reshape(db, t, h, n_sel * MOBA_BLOCK, dh)
        v_sel = cache_v[layer, phys, head].reshape(db, t, h, n_sel * MOBA_BLOCK, dh)
        scores.append(jnp.einsum('bthd,bthkd->bthk', q, k_sel, preferred_element_type=F32) * scale)
        values.append(('bthk,bthkd->bthd', v_sel))
    r0 = PAST_LEN - nb_past * MOBA_BLOCK
    if r0 > 0:
        own_pages = page_table[:, nb_past * ppb:]
        k_own = cache_k[layer, own_pages].transpose(0, 2, 1, 3, 4).reshape(db, h, r0, dh)
        v_own = cache_v[layer, own_pages].transpose(0, 2, 1, 3, 4).reshape(db, h, r0, dh)
        scores.append(jnp.einsum('bthd,bhkd->bthk', q, k_own, preferred_element_type=F32) * scale)
        values.append(('bthk,bhkd->bthd', v_own))
    causal = jnp.arange(t)[:, None] >= jnp.arange(t)[None, :]
    s_new = jnp.einsum('bthd,bshd->bths', q, k_new, preferred_element_type=F32) * scale
    scores.append(jnp.where(causal[None, :, None, :], s_new, -jnp.inf))
    values.append(('bths,bshd->bthd', v_new))
    sizes = [sc.shape[-1] for sc in scores]
    p = jax.nn.softmax(jnp.concatenate(scores, axis=-1), axis=-1).astype(q.dtype)
    out = None
    off = 0
    for (spec, val), n in zip(values, sizes):
        part = jnp.einsum(spec, p[..., off:off + n], val)
        out = part if out is None else out + part
        off += n
    return out.reshape(db, t, h * dh)


def conformer_conv(xn, buf, w_pw1, b_pw1, w_dw, b_dw, ln_g, ln_b, w_pw2, b_pw2):
    a = xn @ w_pw1 + b_pw1
    u = a[..., :D_MODEL] * jax.nn.sigmoid(a[..., D_MODEL:])
    full = jnp.concatenate([buf, u], axis=1)
    y = lax.conv_general_dilated(full, w_dw[:, None, :], window_strides=(1,), padding='VALID',
                                 dimension_numbers=('NWC', 'WIO', 'NWC'),
                                 feature_group_count=D_MODEL) + b_dw
    y = jax.nn.silu(layer_norm(y, ln_g, ln_b))
    return y @ w_pw2 + b_pw2, full[:, full.shape[1] - (CONV_WIDTH - 1):]


def hier_moe(xn, w_gr, b_gr, w_er, b_er, w_gate, w_up, w_down):
    lead = xn.shape[:-1]
    x2 = xn.reshape(-1, D_MODEL)
    n = x2.shape[0]
    g_prob = jax.nn.softmax((x2 @ w_gr + b_gr).astype(F32), axis=-1)
    g_w, g_idx = lax.top_k(g_prob, 1)
    e_logits = (x2 @ w_er + b_er).astype(F32).reshape(n, N_GROUPS, EXPERTS_PER_GROUP)
    e_prob = jax.nn.softmax(e_logits[jnp.arange(n), g_idx[:, 0]], axis=-1)
    e_w, e_idx = lax.top_k(e_prob, EXPERT_TOPK)
    comb = g_w * e_w / jnp.sum(e_w, axis=-1, keepdims=True)
    eid = g_idx * EXPERTS_PER_GROUP + e_idx
    gates = jnp.einsum('nk,nke->ne', comb, jax.nn.one_hot(eid, N_EXPERTS, dtype=F32))
    hdn = jax.nn.silu(jnp.einsum('nd,edf->nef', x2, w_gate)) * jnp.einsum('nd,edf->nef', x2, w_up)
    y = jnp.einsum('nef,efd->nd', hdn * gates[:, :, None].astype(hdn.dtype), w_down)
    return y.reshape(*lead, D_MODEL)


def setup_inputs(seed: int = 0) -> dict:
    key = jax.random.key(seed)
    ks = jax.random.split(key, 26)
    n_pages = PAST_LEN // PAGE_SIZE
    n_phys = (5 * DEC_BATCH * n_pages + 3) // 4
    D = D_MODEL

    def nrm(k, shape, scale=1.0):
        return jax.random.normal(k, shape, F32) * scale

    page_table = jax.random.permutation(ks[4], n_phys)[:DEC_BATCH * n_pages].reshape(DEC_BATCH, n_pages).astype(jnp.int32)
    return {
        'x_prompt': nrm(ks[0], (BATCH, SEQ, D)),
        'x_sample': nrm(ks[1], (DEC_BATCH, DEC_SEQ, D)),
        'cache_k': nrm(ks[2], (N_ATTN_LAYERS, n_phys, N_HEADS, PAGE_SIZE, HEAD_DIM)),
        'cache_v': nrm(ks[3], (N_ATTN_LAYERS, n_phys, N_HEADS, PAGE_SIZE, HEAD_DIM)),
        'page_table': page_table,
        'state_conv': nrm(ks[5], (N_CONV_LAYERS, DEC_BATCH, CONV_WIDTH - 1, D)),
        'norm_mix': 1.0 + nrm(ks[6], (DEPTH, D), 0.01),
        'norm_ffn': 1.0 + nrm(ks[7], (DEPTH, D), 0.01),
        'norm_final': 1.0 + nrm(ks[8], (D,), 0.01),
        'w_qkv': nrm(ks[9], (N_ATTN_LAYERS, D, 3 * D), D ** -0.5),
        'w_o': nrm(ks[10], (N_ATTN_LAYERS, D, D), D ** -0.5),
        'w_pw1': nrm(ks[11], (N_CONV_LAYERS, D, 2 * D), D ** -0.5),
        'b_pw1': nrm(ks[12], (N_CONV_LAYERS, 2 * D), 0.01),
        'w_dw': nrm(ks[13], (N_CONV_LAYERS, CONV_WIDTH, D), CONV_WIDTH ** -0.5),
        'b_dw': nrm(ks[14], (N_CONV_LAYERS, D), 0.01),
        'conv_ln_g': 1.0 + nrm(ks[15], (N_CONV_LAYERS, D), 0.01),
        'conv_ln_b': nrm(ks[16], (N_CONV_LAYERS, D), 0.01),
        'w_pw2': nrm(ks[17], (N_CONV_LAYERS, D, D), D ** -0.5),
        'b_pw2': nrm(ks[18], (N_CONV_LAYERS, D), 0.01),
        'w_group_router': nrm(ks[19], (DEPTH, D, N_GROUPS), D ** -0.5),
        'b_group_router': nrm(ks[20], (DEPTH, N_GROUPS), 0.01),
        'w_expert_router': nrm(ks[21], (DEPTH, D, N_EXPERTS), D ** -0.5),
        'b_expert_router': nrm(ks[22], (DEPTH, N_EXPERTS), 0.01),
        'w_exp_gate': nrm(ks[23], (DEPTH, N_EXPERTS, D, D_EXPERT), D ** -0.5),
        'w_exp_up': nrm(ks[24], (DEPTH, N_EXPERTS, D, D_EXPERT), D ** -0.5),
        'w_exp_down': nrm(ks[25], (DEPTH, N_EXPERTS, D_EXPERT, D), D_EXPERT ** -0.5),
    }


def reference(x_prompt, x_sample, cache_k, cache_v, page_table, state_conv, norm_mix, norm_ffn, norm_final,
              w_qkv, w_o, w_pw1, b_pw1, w_dw, b_dw, conv_ln_g, conv_ln_b, w_pw2, b_pw2,
              w_group_router, b_group_router, w_expert_router, b_expert_router,
              w_exp_gate, w_exp_up, w_exp_down):
    b, s, _ = x_prompt.shape
    db, t, _ = x_sample.shape
    pos_p = jnp.arange(s, dtype=jnp.int32)
    pos_s = PAST_LEN + jnp.arange(t, dtype=jnp.int32)
    hp, hs = x_prompt, x_sample
    k_p, v_p, k_s, v_s, c_p, c_s = [], [], [], [], [], []
    for i in range(DEPTH):
        xp_n = rms_norm(hp, norm_mix[i])
        xs_n = rms_norm(hs, norm_mix[i])
        if i % 2 == 0:
            a = i // 2
            q, k, v = qkv_projection(xp_n, w_qkv[a], pos_p)
            hp = hp + moba_prompt(q, k, v) @ w_o[a]
            k_p.append(k.reshape(b, s // PAGE_SIZE, PAGE_SIZE, N_HEADS, HEAD_DIM).transpose(0, 1, 3, 2, 4))
            v_p.append(v.reshape(b, s // PAGE_SIZE, PAGE_SIZE, N_HEADS, HEAD_DIM).transpose(0, 1, 3, 2, 4))
            q2, k2, v2 = qkv_projection(xs_n, w_qkv[a], pos_s)
            hs = hs + moba_sample(q2, k2, v2, cache_k, cache_v, page_table, a) @ w_o[a]
            k_s.append(k2.transpose(0, 2, 1, 3))
            v_s.append(v2.transpose(0, 2, 1, 3))
        else:
            c = i // 2
            conv_w = (w_pw1[c], b_pw1[c], w_dw[c], b_dw[c], conv_ln_g[c], conv_ln_b[c], w_pw2[c], b_pw2[c])
            mp, bp = conformer_conv(xp_n, jnp.zeros((b, CONV_WIDTH - 1, D_MODEL), xp_n.dtype), *conv_w)
            ms, bs = conformer_conv(xs_n, state_conv[c], *conv_w)
            hp = hp + mp
            hs = hs + ms
            c_p.append(bp)
            c_s.append(bs)
        moe_w = (w_group_router[i], b_group_router[i], w_expert_router[i], b_expert_router[i],
                 w_exp_gate[i], w_exp_up[i], w_exp_down[i])
        hp = hp + hier_moe(rms_norm(hp, norm_ffn[i]), *moe_w)
        hs = hs + hier_moe(rms_norm(hs, norm_ffn[i]), *moe_w)
    y_prompt = rms_norm(hp, norm_final)
    y_sample = rms_norm(hs, norm_final)
    return (y_prompt, y_sample, jnp.stack(k_p), jnp.stack(v_p), jnp.stack(k_s), jnp.stack(v_s), jnp.stack(c_p), jnp.stack(c_s))
```

```python
import functools

import jax
import jax.numpy as jnp
from jax import lax
from jax.experimental import pallas as pl
from jax.experimental.pallas import tpu as pltpu

D_MODEL = 1024
N_HEADS = 16
HEAD_DIM = D_MODEL // N_HEADS
N_PAIRS = N_HEADS // 2
ROT_DIM = HEAD_DIM // 4
ROPE_THETA = 500000.0
MOBA_BLOCK = 256
MOBA_TOPK = 3
PAGE_SIZE = 128
PAGES_PER_BLOCK = MOBA_BLOCK // PAGE_SIZE
CONV_WIDTH = 31
N_GROUPS = 4
EXPERTS_PER_GROUP = 8
N_EXPERTS = N_GROUPS * EXPERTS_PER_GROUP
D_EXPERT = 256
NORM_EPS = 1e-6
LANES = 128
NEG_BIG = -1e30
VMEM_LIMIT = 56 * 1024 * 1024
F32 = jnp.float32
BF16 = jnp.bfloat16
HIGHEST = lax.Precision.HIGHEST
CONTRACT_LAST = (((1,), (1,)), ((), ()))


def _params(*sem):
    return pltpu.CompilerParams(dimension_semantics=sem, vmem_limit_bytes=VMEM_LIMIT)


def _rms(x, g):
    return x * lax.rsqrt(jnp.mean(x * x, axis=-1, keepdims=True) + NORM_EPS) * g


def _rope(x, cos, sin_up, sin_dn):
    reps = x.shape[1] // LANES
    c = jnp.tile(cos, (1, reps))
    su = jnp.tile(sin_up, (1, reps))
    sd = jnp.tile(sin_dn, (1, reps))
    half = ROT_DIM // 2
    return x * c + pltpu.roll(x, half, 1) * su + pltpu.roll(x, x.shape[1] - half, 1) * sd


def _rope_tables(pos):
    half = ROT_DIM // 2
    inv_freq = jnp.power(jnp.asarray(ROPE_THETA, F32), -jnp.arange(0, ROT_DIM, 2, dtype=F32) / ROT_DIM)
    ang = pos.astype(F32)[:, None] * inv_freq[None, :]
    cos, sin = jnp.cos(ang), jnp.sin(ang)
    rows = pos.shape[0]
    ones = jnp.ones((rows, HEAD_DIM - ROT_DIM), F32)
    zeros = jnp.zeros((rows, HEAD_DIM - ROT_DIM), F32)
    zh = jnp.zeros((rows, half), F32)
    cos_h = jnp.concatenate([cos, cos, ones], axis=1)
    up_h = jnp.concatenate([zh, sin, zeros], axis=1)
    dn_h = jnp.concatenate([-sin, zh, zeros], axis=1)
    reps = LANES // HEAD_DIM
    return jnp.tile(cos_h, (1, reps)), jnp.tile(up_h, (1, reps)), jnp.tile(dn_h, (1, reps))


QKV_ROWS = 256


def _qkv_prompt_kernel(x_ref, g_ref, w_ref, cos_ref, su_ref, sd_ref,
                       q2_ref, k2_ref, v2_ref, kp_ref, vp_ref, km_ref):
    xn = _rms(x_ref[0], g_ref[...]).astype(BF16)
    acc = jnp.dot(xn, w_ref[...], preferred_element_type=F32)
    cos, su, sd = cos_ref[...], su_ref[...], sd_ref[...]
    q = _rope(acc[:, :D_MODEL], cos, su, sd)
    k = _rope(acc[:, D_MODEL:2 * D_MODEL], cos, su, sd)
    v = acc[:, 2 * D_MODEL:]
    for pr in range(N_PAIRS):
        cols = slice(pr * LANES, (pr + 1) * LANES)
        q2_ref[0, pr] = q[:, cols]
        k2_ref[0, pr] = k[:, cols].astype(BF16)
        v2_ref[0, pr] = v[:, cols].astype(BF16)
    kt = k.T.reshape(N_HEADS, HEAD_DIM, QKV_ROWS)
    vt = v.T.reshape(N_HEADS, HEAD_DIM, QKV_ROWS)
    for p in range(QKV_ROWS // PAGE_SIZE):
        rows = slice(p * PAGE_SIZE, (p + 1) * PAGE_SIZE)
        kp_ref[0, p] = kt[:, :, rows]
        vp_ref[0, p] = vt[:, :, rows]
    km_ref[0, 0] = jnp.sum(k, axis=0, keepdims=True) * (1.0 / MOBA_BLOCK)


def _qkv_prompt(x, g, w_bf16):
    b, s, d = x.shape
    nt = s // QKV_ROWS
    cos, su, sd = _rope_tables(jnp.arange(s, dtype=jnp.int32))
    tab_spec = pl.BlockSpec((QKV_ROWS, LANES), lambda bi, si: (si, 0))
    pair_spec = pl.BlockSpec((1, N_PAIRS, QKV_ROWS, LANES), lambda bi, si: (bi, 0, si, 0))
    page_spec = pl.BlockSpec((1, QKV_ROWS // PAGE_SIZE, N_HEADS, HEAD_DIM, PAGE_SIZE),
                             lambda bi, si: (bi, si, 0, 0, 0))
    return pl.pallas_call(
        _qkv_prompt_kernel,
        grid=(b, nt),
        in_specs=[
            pl.BlockSpec((1, QKV_ROWS, d), lambda bi, si: (bi, si, 0)),
            pl.BlockSpec((1, d), lambda bi, si: (0, 0)),
            pl.BlockSpec((d, 3 * d), lambda bi, si: (0, 0)),
            tab_spec, tab_spec, tab_spec,
        ],
        out_specs=[
            pair_spec, pair_spec, pair_spec, page_spec, page_spec,
            pl.BlockSpec((1, 1, 1, d), lambda bi, si: (bi, si, 0, 0)),
        ],
        out_shape=[
            jax.ShapeDtypeStruct((b, N_PAIRS, s, LANES), F32),
            jax.ShapeDtypeStruct((b, N_PAIRS, s, LANES), BF16),
            jax.ShapeDtypeStruct((b, N_PAIRS, s, LANES), BF16),
            jax.ShapeDtypeStruct((b, s // PAGE_SIZE, N_HEADS, HEAD_DIM, PAGE_SIZE), F32),
            jax.ShapeDtypeStruct((b, s // PAGE_SIZE, N_HEADS, HEAD_DIM, PAGE_SIZE), F32),
            jax.ShapeDtypeStruct((b, nt, 1, d), F32),
        ],
        compiler_params=_params("parallel", "parallel"),
        name="qkv_prompt",
    )(x, g.reshape(1, d), w_bf16, cos, su, sd)


def _topk_block_bias(gate, blk, valid):
    n_blk = gate.shape[1]
    g = jnp.where(valid, gate, -jnp.inf)
    sel = jnp.zeros(gate.shape, jnp.bool_)
    for _ in range(MOBA_TOPK):
        m = jnp.max(g, axis=1, keepdims=True)
        hit = jnp.logical_and(g == m, m > -jnp.inf)
        idx = jnp.min(jnp.where(hit, blk, n_blk), axis=1, keepdims=True)
        pick = blk == idx
        sel = jnp.logical_or(sel, pick)
        g = jnp.where(pick, -jnp.inf, g)
    return jnp.where(sel, 0.0, NEG_BIG)


def _moba_prompt_kernel(q_ref, k_ref, v_ref, km_ref, o_ref):
    qi = pl.program_id(2)
    tq = q_ref.shape[2]
    n_blk = km_ref.shape[2]
    q = q_ref[0, 0]
    lane = lax.broadcasted_iota(jnp.int32, (1, LANES), 1)
    is_a = lane < HEAD_DIM
    qa = jnp.where(is_a, q, 0.0)
    qb = jnp.where(is_a, 0.0, q)
    km = km_ref[0, 0]
    blk = lax.broadcasted_iota(jnp.int32, (tq, n_blk), 1)
    valid = blk < qi
    gate_a = lax.dot_general(qa, km, CONTRACT_LAST, precision=HIGHEST, preferred_element_type=F32)
    gate_b = lax.dot_general(qb, km, CONTRACT_LAST, precision=HIGHEST, preferred_element_type=F32)
    bias_a = _topk_block_bias(gate_a, blk, valid)
    bias_b = _topk_block_bias(gate_b, blk, valid)
    scale = HEAD_DIM ** -0.5
    qa_s = (qa * scale).astype(BF16)
    qb_s = (qb * scale).astype(BF16)

    def scores(kj):
        sa = lax.dot_general(qa_s, kj, CONTRACT_LAST, preferred_element_type=F32)
        sb = lax.dot_general(qb_s, kj, CONTRACT_LAST, preferred_element_type=F32)
        return sa, sb

    def weighted(pa, pb, vj):
        oa = jnp.dot(pa.astype(BF16), vj, preferred_element_type=F32)
        ob = jnp.dot(pb.astype(BF16), vj, preferred_element_type=F32)
        return jnp.where(is_a, oa, ob)

    d0 = pl.multiple_of(qi * tq, tq)
    kd = k_ref[0, 0, pl.ds(d0, tq), :]
    vd = v_ref[0, 0, pl.ds(d0, tq), :]
    row = lax.broadcasted_iota(jnp.int32, (tq, tq), 0)
    col = lax.broadcasted_iota(jnp.int32, (tq, tq), 1)
    causal = row >= col
    sa, sb = scores(kd)
    sa = jnp.where(causal, sa, NEG_BIG)
    sb = jnp.where(causal, sb, NEG_BIG)
    ma = jnp.max(sa, axis=1, keepdims=True)
    mb = jnp.max(sb, axis=1, keepdims=True)
    pa = jnp.exp(sa - ma)
    pb = jnp.exp(sb - mb)
    la = jnp.sum(pa, axis=1, keepdims=True)
    lb = jnp.sum(pb, axis=1, keepdims=True)
    acc = weighted(pa, pb, vd)

    def body(j, carry):
        ma, la, mb, lb, acc = carry
        j0 = pl.multiple_of(j * MOBA_BLOCK, MOBA_BLOCK)
        kj = k_ref[0, 0, pl.ds(j0, MOBA_BLOCK), :]
        vj = v_ref[0, 0, pl.ds(j0, MOBA_BLOCK), :]
        ca = jnp.sum(jnp.where(blk == j, bias_a, 0.0), axis=1, keepdims=True)
        cb = jnp.sum(jnp.where(blk == j, bias_b, 0.0), axis=1, keepdims=True)
        sa, sb = scores(kj)
        sa = sa + ca
        sb = sb + cb
        mna = jnp.maximum(ma, jnp.max(sa, axis=1, keepdims=True))
        mnb = jnp.maximum(mb, jnp.max(sb, axis=1, keepdims=True))
        al_a = jnp.exp(ma - mna)
        al_b = jnp.exp(mb - mnb)
        pa = jnp.exp(sa - mna)
        pb = jnp.exp(sb - mnb)
        la = al_a * la + jnp.sum(pa, axis=1, keepdims=True)
        lb = al_b * lb + jnp.sum(pb, axis=1, keepdims=True)
        acc = jnp.where(is_a, al_a, al_b) * acc + weighted(pa, pb, vj)
        return mna, la, mnb, lb, acc

    ma, la, mb, lb, acc = lax.fori_loop(0, qi, body, (ma, la, mb, lb, acc))
    o_ref[0] = (acc / jnp.where(is_a, la, lb)).astype(o_ref.dtype)


def _moba_prompt(q2, k2, v2, kmean):
    b, n_pairs, s, _ = q2.shape
    tq = MOBA_BLOCK
    n_blk = s // MOBA_BLOCK
    seq_spec = pl.BlockSpec((1, 1, s, LANES), lambda bi, pi, qi: (bi, pi, 0, 0))
    return pl.pallas_call(
        _moba_prompt_kernel,
        grid=(b, n_pairs, s // tq),
        in_specs=[
            pl.BlockSpec((1, 1, tq, LANES), lambda bi, pi, qi: (bi, pi, qi, 0)),
            seq_spec, seq_spec,
            pl.BlockSpec((1, 1, n_blk, LANES), lambda bi, pi, qi: (bi, pi, 0, 0)),
        ],
        out_specs=pl.BlockSpec((1, tq, LANES), lambda bi, pi, qi: (bi, qi, pi)),
        out_shape=jax.ShapeDtypeStruct((b, s, n_pairs * LANES), BF16),
        compiler_params=_params("parallel", "parallel", "parallel"),
        name="moba_prompt",
    )(q2, k2, v2, kmean)


def _linear_residual_kernel(x_ref, w_ref, r_ref, o_ref):
    o_ref[...] = r_ref[...] + jnp.dot(x_ref[...], w_ref[...], preferred_element_type=F32)


def _linear_residual(x_bf16, w_bf16, resid, rows):
    n, k = x_bf16.shape
    m = w_bf16.shape[1]
    return pl.pallas_call(
        _linear_residual_kernel,
        grid=(n // rows,),
        in_specs=[
            pl.BlockSpec((rows, k), lambda i: (i, 0)),
            pl.BlockSpec((k, m), lambda i: (0, 0)),
            pl.BlockSpec((rows, m), lambda i: (i, 0)),
        ],
        out_specs=pl.BlockSpec((rows, m), lambda i: (i, 0)),
        out_shape=jax.ShapeDtypeStruct((n, m), F32),
        compiler_params=_params("parallel"),
        name="linear_residual",
    )(x_bf16, w_bf16, resid)


MOE_EXPERTS_PER_STEP = 4
ROUTER_GROUP_LANE0 = N_EXPERTS
EXPERT_GROUP_SHIFT = EXPERTS_PER_GROUP.bit_length() - 1
assert 1 << EXPERT_GROUP_SHIFT == EXPERTS_PER_GROUP


def _route(logits):
    lane = lax.broadcasted_iota(jnp.int32, logits.shape, 1)
    is_g = jnp.logical_and(lane >= ROUTER_GROUP_LANE0, lane < ROUTER_GROUP_LANE0 + N_GROUPS)
    lg = jnp.where(is_g, logits, -jnp.inf)
    mg = jnp.max(lg, axis=1, keepdims=True)
    zg = jnp.sum(jnp.exp(lg - mg), axis=1, keepdims=True)
    g_w = 1.0 / zg
    g_idx = jnp.min(jnp.where(lg == mg, lane - ROUTER_GROUP_LANE0, N_GROUPS), axis=1, keepdims=True)
    in_grp = jnp.logical_and(lane < N_EXPERTS, jnp.right_shift(lane, EXPERT_GROUP_SHIFT) == g_idx)
    le = jnp.where(in_grp, logits, -jnp.inf)
    m1 = jnp.max(le, axis=1, keepdims=True)
    ze = jnp.sum(jnp.exp(le - m1), axis=1, keepdims=True)
    idx1 = jnp.min(jnp.where(le == m1, lane, LANES), axis=1, keepdims=True)
    le2 = jnp.where(lane == idx1, -jnp.inf, le)
    m2 = jnp.max(le2, axis=1, keepdims=True)
    idx2 = jnp.min(jnp.where(le2 == m2, lane, LANES), axis=1, keepdims=True)
    w1 = 1.0 / ze
    w2 = jnp.exp(m2 - m1) / ze
    c1 = g_w * w1 / (w1 + w2)
    c2 = g_w * w2 / (w1 + w2)
    return jnp.where(lane == idx1, c1, 0.0) + jnp.where(lane == idx2, c2, 0.0)


def _moe_kernel(h_ref, g_ref, wr_ref, br_ref, wg_ref, wu_ref, wd_ref, gf_ref, o_ref,
                xn_scr, gate_scr, acc_scr, *, final_norm):
    e = pl.program_id(1)

    @pl.when(e == 0)
    def _():
        h = h_ref[...]
        xn = _rms(h, g_ref[...])
        xn_scr[...] = xn.astype(BF16)
        logits = jnp.dot(xn, wr_ref[...], precision=HIGHEST, preferred_element_type=F32) + br_ref[...]
        gate_scr[...] = _route(logits)
        acc_scr[...] = h

    xn = xn_scr[...]
    gates = gate_scr[...]
    lane = lax.broadcasted_iota(jnp.int32, gates.shape, 1)
    acc = acc_scr[...]
    for c in range(MOE_EXPERTS_PER_STEP):
        eid = e * MOE_EXPERTS_PER_STEP + c
        gcol = jnp.sum(jnp.where(lane == eid, gates, 0.0), axis=1, keepdims=True)
        hg = jnp.dot(xn, wg_ref[c], preferred_element_type=F32)
        hu = jnp.dot(xn, wu_ref[c], preferred_element_type=F32)
        hdn = (hg * jax.nn.sigmoid(hg)) * hu * gcol
        acc = acc + jnp.dot(hdn.astype(BF16), wd_ref[c], preferred_element_type=F32)
    acc_scr[...] = acc

    @pl.when(e == pl.num_programs(1) - 1)
    def _():
        if final_norm:
            o_ref[...] = _rms(acc, gf_ref[...])
        else:
            o_ref[...] = acc


def _moe(h, g, w_router, b_router, wg, wu, wd, g_final, rows, final_norm):
    n, d = h.shape
    steps = N_EXPERTS // MOE_EXPERTS_PER_STEP
    row_spec = pl.BlockSpec((rows, d), lambda i, e: (i, 0))
    vec_spec = pl.BlockSpec((1, d), lambda i, e: (0, 0))
    return pl.pallas_call(
        functools.partial(_moe_kernel, final_norm=final_norm),
        grid=(n // rows, steps),
        in_specs=[
            row_spec, vec_spec,
            pl.BlockSpec((d, LANES), lambda i, e: (0, 0)),
            pl.BlockSpec((1, LANES), lambda i, e: (0, 0)),
            pl.BlockSpec((MOE_EXPERTS_PER_STEP, d, D_EXPERT), lambda i, e: (e, 0, 0)),
            pl.BlockSpec((MOE_EXPERTS_PER_STEP, d, D_EXPERT), lambda i, e: (e, 0, 0)),
            pl.BlockSpec((MOE_EXPERTS_PER_STEP, D_EXPERT, d), lambda i, e: (e, 0, 0)),
            vec_spec,
        ],
        out_specs=row_spec,
        out_shape=jax.ShapeDtypeStruct((n, d), F32),
        scratch_shapes=[
            pltpu.VMEM((rows, d), BF16),
            pltpu.VMEM((rows, LANES), F32),
            pltpu.VMEM((rows, d), F32),
        ],
        compiler_params=_params("parallel", "arbitrary"),
        name="hier_moe",
    )(h, g.reshape(1, d), w_router, b_router, wg, wu, wd, g_final.reshape(1, d))


def _router_params(w_gr, b_gr, w_er, b_er):
    d = w_gr.shape[0]
    pad = LANES - N_EXPERTS - N_GROUPS
    w = jnp.concatenate([w_er, w_gr, jnp.zeros((d, pad), F32)], axis=1)
    b = jnp.concatenate([b_er, b_gr, jnp.zeros((pad,), F32)]).reshape(1, LANES)
    return w, b


CONV_ROWS = 512
CONV_HALO = 32
CONV_ROW_CHUNK = 64
CONV_LANE_CHUNK = 256


def _layer_norm_silu(y, g, b):
    mu = jnp.mean(y, axis=-1, keepdims=True)
    yc = y - mu
    var = jnp.mean(yc * yc, axis=-1, keepdims=True)
    z = yc * lax.rsqrt(var + NORM_EPS) * g + b
    return z * jax.nn.sigmoid(z)


def _conv_prompt_kernel(x_ref, g_ref, w1_ref, b1_ref, wdw_ref, bdw_ref, lng_ref, lnb_ref, w2_ref, b2_ref,
                        o_ref, cb_ref, u_scr, y_scr):
    si = pl.program_id(1)
    rows = x_ref.shape[1]

    @pl.when(si == 0)
    def _():
        u_scr[0:CONV_HALO, :] = jnp.zeros((CONV_HALO, D_MODEL), F32)

    @pl.when(si > 0)
    def _():
        u_scr[0:CONV_HALO, :] = u_scr[rows:rows + CONV_HALO, :]

    x = x_ref[0]
    xn = _rms(x, g_ref[...]).astype(BF16)
    a = jnp.dot(xn, w1_ref[...], preferred_element_type=F32) + b1_ref[...]
    u_scr[CONV_HALO:CONV_HALO + rows, :] = a[:, :D_MODEL] * jax.nn.sigmoid(a[:, D_MODEL:])

    lead = CONV_HALO - (CONV_WIDTH - 1)
    for r0 in range(0, rows, CONV_ROW_CHUNK):
        for c0 in range(0, D_MODEL, CONV_LANE_CHUNK):
            cols = slice(c0, c0 + CONV_LANE_CHUNK)
            y = jnp.broadcast_to(bdw_ref[:, cols], (CONV_ROW_CHUNK, CONV_LANE_CHUNK))
            for w in range(CONV_WIDTH):
                start = r0 + lead + w
                y = y + u_scr[start:start + CONV_ROW_CHUNK, cols] * wdw_ref[w:w + 1, cols]
            y_scr[r0:r0 + CONV_ROW_CHUNK, cols] = y

    z = _layer_norm_silu(y_scr[...], lng_ref[...], lnb_ref[...])
    o_ref[0] = x + jnp.dot(z.astype(BF16), w2_ref[...], preferred_element_type=F32) + b2_ref[...]

    @pl.when(si == pl.num_programs(1) - 1)
    def _():
        cb_ref[0] = u_scr[rows + lead:rows + CONV_HALO, :]


def _conv_prompt(x, g, w1, b1, wdw, bdw, lng, lnb, w2, b2):
    b, s, d = x.shape
    vec = lambda n: pl.BlockSpec((1, n), lambda bi, si: (0, 0))
    return pl.pallas_call(
        _conv_prompt_kernel,
        grid=(b, s // CONV_ROWS),
        in_specs=[
            pl.BlockSpec((1, CONV_ROWS, d), lambda bi, si: (bi, si, 0)),
            vec(d),
            pl.BlockSpec((d, 2 * d), lambda bi, si: (0, 0)),
            vec(2 * d),
            pl.BlockSpec((CONV_WIDTH, d), lambda bi, si: (0, 0)),
            vec(d), vec(d), vec(d),
            pl.BlockSpec((d, d), lambda bi, si: (0, 0)),
            vec(d),
        ],
        out_specs=[
            pl.BlockSpec((1, CONV_ROWS, d), lambda bi, si: (bi, si, 0)),
            pl.BlockSpec((1, CONV_WIDTH - 1, d), lambda bi, si: (bi, 0, 0)),
        ],
        out_shape=[
            jax.ShapeDtypeStruct((b, s, d), F32),
            jax.ShapeDtypeStruct((b, CONV_WIDTH - 1, d), F32),
        ],
        scratch_shapes=[
            pltpu.VMEM((CONV_HALO + CONV_ROWS, d), F32),
            pltpu.VMEM((CONV_ROWS, d), F32),
        ],
        compiler_params=_params("parallel", "arbitrary"),
        name="conformer_conv_prompt",
    )(x, g.reshape(1, d), w1, b1.reshape(1, 2 * d), wdw, bdw.reshape(1, d), lng.reshape(1, d),
      lnb.reshape(1, d), w2, b2.reshape(1, d))


def _conv_sample_kernel(x_ref, st_ref, g_ref, w1_ref, b1_ref, wdw_ref, bdw_ref, lng_ref, lnb_ref, w2_ref, b2_ref,
                        o_ref, u_ref):
    t_new, db, d = x_ref.shape
    n_state = st_ref.shape[0]
    for t in range(t_new):
        xn = _rms(x_ref[t], g_ref[...]).astype(BF16)
        a = jnp.dot(xn, w1_ref[...], preferred_element_type=F32) + b1_ref[...]
        u_ref[t] = a[:, :D_MODEL] * jax.nn.sigmoid(a[:, D_MODEL:])
    for t in range(t_new):
        y = jnp.broadcast_to(bdw_ref[...], (db, d))
        for w in range(CONV_WIDTH):
            r = t + w
            row = st_ref[r] if r < n_state else u_ref[r - n_state]
            y = y + row * wdw_ref[w:w + 1, :]
        z = _layer_norm_silu(y, lng_ref[...], lnb_ref[...])
        o_ref[t] = x_ref[t] + jnp.dot(z.astype(BF16), w2_ref[...], preferred_element_type=F32) + b2_ref[...]


def _conv_sample(x_tb, state_tb, g, w1, b1, wdw, bdw, lng, lnb, w2, b2):
    t_new, db, d = x_tb.shape
    return pl.pallas_call(
        _conv_sample_kernel,
        out_shape=[
            jax.ShapeDtypeStruct((t_new, db, d), F32),
            jax.ShapeDtypeStruct((t_new, db, d), F32),
        ],
        compiler_params=pltpu.CompilerParams(vmem_limit_bytes=VMEM_LIMIT),
        name="conformer_conv_sample",
    )(x_tb, state_tb, g.reshape(1, d), w1, b1.reshape(1, 2 * d), wdw, bdw.reshape(1, d), lng.reshape(1, d),
      lnb.reshape(1, d), w2, b2.reshape(1, d))


def _qkv_sample_kernel(x_ref, g_ref, w_ref, cos_ref, su_ref, sd_ref, q_ref, k_ref, v_ref):
    xn = _rms(x_ref[...], g_ref[...]).astype(BF16)
    acc = jnp.dot(xn, w_ref[...], preferred_element_type=F32)
    cos, su, sd = cos_ref[...], su_ref[...], sd_ref[...]
    q_ref[...] = _rope(acc[:, :D_MODEL], cos, su, sd)
    k_ref[...] = _rope(acc[:, D_MODEL:2 * D_MODEL], cos, su, sd)
    v_ref[...] = acc[:, 2 * D_MODEL:]


def _qkv_sample(x, g, w_bf16, pos):
    n, d = x.shape
    cos, su, sd = _rope_tables(pos)
    out = jax.ShapeDtypeStruct((n, d), F32)
    return pl.pallas_call(
        _qkv_sample_kernel,
        out_shape=[out, out, out],
        compiler_params=pltpu.CompilerParams(vmem_limit_bytes=VMEM_LIMIT),
        name="qkv_sample",
    )(x, g.reshape(1, d), w_bf16, cos, su, sd)


SEL_PAGES_PER_STEP = 8
Q_PAD = 8
N_SEL_PAGES = MOBA_TOPK * PAGES_PER_BLOCK


def _past_gate_kernel(pt_ref, q_ref, *rest):
    page_refs = rest[:SEL_PAGES_PER_STEP]
    sel_ref = rest[SEL_PAGES_PER_STEP]
    km_scr = rest[SEL_PAGES_PER_STEP + 1]
    i = pl.program_id(1)
    n_blk = pl.num_programs(1) * (SEL_PAGES_PER_STEP // PAGES_PER_BLOCK)
    lane3 = lax.broadcasted_iota(jnp.int32, km_scr.shape, 2)

    @pl.when(i == 0)
    def _():
        km_scr[...] = jnp.zeros(km_scr.shape, F32)

    km = km_scr[...]
    for c in range(SEL_PAGES_PER_STEP // PAGES_PER_BLOCK):
        tot = page_refs[c * PAGES_PER_BLOCK][0, 0]
        for p in range(1, PAGES_PER_BLOCK):
            tot = tot + page_refs[c * PAGES_PER_BLOCK + p][0, 0]
        mean = jnp.sum(tot, axis=2, keepdims=True) * (1.0 / MOBA_BLOCK)
        km = jnp.where(lane3 == i * (SEL_PAGES_PER_STEP // PAGES_PER_BLOCK) + c, mean, km)
    km_scr[...] = km

    @pl.when(i == pl.num_programs(1) - 1)
    def _():
        gates = [jnp.dot(q_ref[0, h], km[h], precision=HIGHEST, preferred_element_type=F32)
                 for h in range(N_HEADS)]
        g = jnp.concatenate(gates, axis=0)
        lane = lax.broadcasted_iota(jnp.int32, g.shape, 1)
        g = jnp.where(lane < n_blk, g, -jnp.inf)
        out = jnp.zeros(g.shape, jnp.int32)
        for r in range(MOBA_TOPK):
            m = jnp.max(g, axis=1, keepdims=True)
            idx = jnp.min(jnp.where(g == m, lane, LANES), axis=1, keepdims=True)
            out = jnp.where(lane == r, idx, out)
            g = jnp.where(lane == idx, -jnp.inf, g)
        sel_ref[0] = out


def _past_gate(page_table_flat, q_pad, cache_kt, layer):
    db, h, _, dh = q_pad.shape
    n_pages = page_table_flat.shape[0] // db
    assert n_pages // PAGES_PER_BLOCK <= LANES

    def page_map(r):
        return lambda bi, i, pt: (layer, pt[bi * n_pages + i * SEL_PAGES_PER_STEP + r], 0, 0, 0)

    page_specs = [pl.BlockSpec((1, 1, h, dh, PAGE_SIZE), page_map(r)) for r in range(SEL_PAGES_PER_STEP)]
    grid_spec = pltpu.PrefetchScalarGridSpec(
        num_scalar_prefetch=1,
        grid=(db, n_pages // SEL_PAGES_PER_STEP),
        in_specs=[pl.BlockSpec((1, h, Q_PAD, dh), lambda bi, i, pt: (bi, 0, 0, 0))] + page_specs,
        out_specs=pl.BlockSpec((1, h * Q_PAD, LANES), lambda bi, i, pt: (bi, 0, 0)),
        scratch_shapes=[pltpu.VMEM((h, dh, LANES), F32)],
    )
    return pl.pallas_call(
        _past_gate_kernel,
        grid_spec=grid_spec,
        out_shape=jax.ShapeDtypeStruct((db, h * Q_PAD, LANES), jnp.int32),
        compiler_params=_params("parallel", "arbitrary"),
        name="past_gate",
    )(page_table_flat, q_pad, *([cache_kt] * SEL_PAGES_PER_STEP))


def _moba_sample_kernel(pt_ref, sel_ref, q_ref, kn_ref, vn_ref, *rest):
    k_refs = rest[:N_SEL_PAGES]
    v_refs = rest[N_SEL_PAGES:2 * N_SEL_PAGES]
    o_ref = rest[2 * N_SEL_PAGES]
    acc_scr = rest[2 * N_SEL_PAGES + 1]
    t = pl.program_id(2)
    t_new = pl.num_programs(2)

    @pl.when(t == 0)
    def _():
        acc_scr[...] = jnp.zeros(acc_scr.shape, F32)

    qs = (q_ref[0, 0] * HEAD_DIM ** -0.5).astype(BF16)
    s_sel = [jnp.dot(qs, kr[0, 0, 0].astype(BF16), preferred_element_type=F32) for kr in k_refs]
    s_new = lax.dot_general(qs, kn_ref[0, 0].astype(BF16), CONTRACT_LAST, preferred_element_type=F32)
    row = lax.broadcasted_iota(jnp.int32, (Q_PAD, Q_PAD), 0)
    col = lax.broadcasted_iota(jnp.int32, (Q_PAD, Q_PAD), 1)
    s_new = jnp.where(jnp.logical_and(col <= row, col < t_new), s_new, NEG_BIG)
    m = jnp.max(s_new, axis=1, keepdims=True)
    for s in s_sel:
        m = jnp.maximum(m, jnp.max(s, axis=1, keepdims=True))
    p_new = jnp.exp(s_new - m)
    l = jnp.sum(p_new, axis=1, keepdims=True)
    out = jnp.dot(p_new.astype(BF16), vn_ref[0, 0].astype(BF16), preferred_element_type=F32)
    for s, vr in zip(s_sel, v_refs):
        p = jnp.exp(s - m)
        l = l + jnp.sum(p, axis=1, keepdims=True)
        out = out + lax.dot_general(p.astype(BF16), vr[0, 0, 0].astype(BF16), CONTRACT_LAST,
                                    preferred_element_type=F32)
    rowi = lax.broadcasted_iota(jnp.int32, (Q_PAD, HEAD_DIM), 0)
    acc = jnp.where(rowi == t, out / l, acc_scr[...])
    acc_scr[...] = acc

    @pl.when(t == t_new - 1)
    def _():
        o_ref[0, 0] = acc


def _moba_sample(page_table_flat, sel_flat, q_pad, kn_pad, vn_pad, cache_kt, cache_vt, layer, t_new):
    db, h, _, dh = q_pad.shape
    n_pages = page_table_flat.shape[0] // db

    def page_map(r):
        n, half = divmod(r, PAGES_PER_BLOCK)

        def index(bi, hi, ti, pt, sel):
            blk = sel[((bi * h + hi) * Q_PAD + ti) * MOBA_TOPK + n]
            return (layer, pt[bi * n_pages + blk * PAGES_PER_BLOCK + half], hi, 0, 0)
        return index

    new_spec = pl.BlockSpec((1, 1, Q_PAD, dh), lambda bi, hi, ti, pt, sel: (bi, hi, 0, 0))
    page_specs = [pl.BlockSpec((1, 1, 1, dh, PAGE_SIZE), page_map(r)) for r in range(N_SEL_PAGES)]
    grid_spec = pltpu.PrefetchScalarGridSpec(
        num_scalar_prefetch=2,
        grid=(db, h, t_new),
        in_specs=[new_spec, new_spec, new_spec] + page_specs + page_specs,
        out_specs=new_spec,
        scratch_shapes=[pltpu.VMEM((Q_PAD, dh), F32)],
    )
    return pl.pallas_call(
        _moba_sample_kernel,
        grid_spec=grid_spec,
        out_shape=jax.ShapeDtypeStruct((db, h, Q_PAD, dh), F32),
        compiler_params=_params("parallel", "parallel", "arbitrary"),
        name="moba_sample",
    )(page_table_flat, sel_flat, q_pad, kn_pad, vn_pad,
      *([cache_kt] * N_SEL_PAGES), *([cache_vt] * N_SEL_PAGES))


def _pad_rows(x_bhtd):
    db, h, t_new, dh = x_bhtd.shape
    return jnp.concatenate([x_bhtd, jnp.zeros((db, h, Q_PAD - t_new, dh), x_bhtd.dtype)], axis=2)


PROMPT_ROWS = 512
MOE_PROMPT_ROWS = 1024


def kernel(x_prompt, x_sample, cache_k, cache_v, page_table, state_conv, norm_mix, norm_ffn, norm_final,
           w_qkv, w_o, w_pw1, b_pw1, w_dw, b_dw, conv_ln_g, conv_ln_b, w_pw2, b_pw2,
           w_group_router, b_group_router, w_expert_router, b_expert_router,
           w_exp_gate, w_exp_up, w_exp_down):
    b, s, d = x_prompt.shape
    db, t_new, _ = x_sample.shape
    assert page_table.shape[1] % PAGES_PER_BLOCK == 0
    past_len = page_table.shape[1] * PAGE_SIZE
    n_p = b * s
    n_s = db * t_new

    def moe(h, layer, rows, final):
        w_r, b_r = _router_params(w_group_router[layer], b_group_router[layer],
                                  w_expert_router[layer], b_expert_router[layer])
        return _moe(h, norm_ffn[layer], w_r, b_r, w_exp_gate[layer].astype(BF16), w_exp_up[layer].astype(BF16),
                    w_exp_down[layer].astype(BF16), norm_final, rows, final)

    wqkv = w_qkv[0].astype(BF16)
    wo = w_o[0].astype(BF16)
    q2, k2, v2, kt_paged, vt_paged, kmean = _qkv_prompt(x_prompt, norm_mix[0], wqkv)
    n_blk = s // MOBA_BLOCK
    kmean = kmean.reshape(b, n_blk, N_PAIRS, LANES).transpose(0, 2, 1, 3)
    attn_p = _moba_prompt(q2, k2, v2, kmean)
    hp = _linear_residual(attn_p.reshape(n_p, d), wo, x_prompt.reshape(n_p, d), PROMPT_ROWS)

    xs = x_sample.reshape(n_s, d)
    pos_s = past_len + jnp.tile(jnp.arange(t_new, dtype=jnp.int32), db)
    qs, ks, vs = _qkv_sample(xs, norm_mix[0], wqkv, pos_s)
    q_pad = _pad_rows(qs.reshape(db, t_new, N_HEADS, HEAD_DIM).transpose(0, 2, 1, 3))
    k_bhtd = ks.reshape(db, t_new, N_HEADS, HEAD_DIM).transpose(0, 2, 1, 3)
    v_bhtd = vs.reshape(db, t_new, N_HEADS, HEAD_DIM).transpose(0, 2, 1, 3)
    pt_flat = page_table.reshape(-1)
    cache_kt = jnp.swapaxes(cache_k, 3, 4)
    cache_vt = jnp.swapaxes(cache_v, 3, 4)
    sel = _past_gate(pt_flat, q_pad, cache_kt, 0)[:, :, :MOBA_TOPK]
    attn_s = _moba_sample(pt_flat, sel.reshape(-1), q_pad, _pad_rows(k_bhtd), _pad_rows(v_bhtd),
                          cache_kt, cache_vt, 0, t_new)
    attn_s = attn_s[:, :, :t_new].transpose(0, 2, 1, 3).reshape(n_s, d).astype(BF16)
    hs = _linear_residual(attn_s, wo, xs, n_s)

    hp = moe(hp, 0, MOE_PROMPT_ROWS, False)
    hs = moe(hs, 0, n_s, False)

    w1 = w_pw1[0].astype(BF16)
    w2 = w_pw2[0].astype(BF16)
    conv_w = (norm_mix[1], w1, b_pw1[0], w_dw[0], b_dw[0], conv_ln_g[0], conv_ln_b[0], w2, b_pw2[0])
    hp, conv_p = _conv_prompt(hp.reshape(b, s, d), *conv_w)
    hs_tb, u_tb = _conv_sample(hs.reshape(db, t_new, d).transpose(1, 0, 2), state_conv[0].transpose(1, 0, 2), *conv_w)
    hs = hs_tb.transpose(1, 0, 2).reshape(n_s, d)
    conv_s = jnp.concatenate([state_conv[0], u_tb.transpose(1, 0, 2)], axis=1)[:, t_new:]

    y_prompt = moe(hp.reshape(n_p, d), 1, MOE_PROMPT_ROWS, True).reshape(b, s, d)
    y_sample = moe(hs, 1, n_s, True).reshape(db, t_new, d)

    return (y_prompt, y_sample, jnp.swapaxes(kt_paged, 3, 4)[None], jnp.swapaxes(vt_paged, 3, 4)[None],
            k_bhtd[None], v_bhtd[None], conv_p[None], conv_s[None])
```

```python
import functools

import jax
import jax.numpy as jnp
from jax import lax
from jax.experimental import pallas as pl
from jax.experimental.pallas import tpu as pltpu

D_MODEL = 1024
N_HEADS = 16
HEAD_DIM = D_MODEL // N_HEADS
N_PAIRS = N_HEADS // 2
ROT_DIM = HEAD_DIM // 4
ROPE_THETA = 500000.0
MOBA_BLOCK = 256
MOBA_TOPK = 3
PAGE_SIZE = 128
PAGES_PER_BLOCK = MOBA_BLOCK // PAGE_SIZE
CONV_WIDTH = 31
N_GROUPS = 4
EXPERTS_PER_GROUP = 8
N_EXPERTS = N_GROUPS * EXPERTS_PER_GROUP
D_EXPERT = 256
NORM_EPS = 1e-6
LANES = 128
SUBLANES = 8
NEG_BIG = -1e30
VMEM_LIMIT = 56 * 1024 * 1024
F32 = jnp.float32
BF16 = jnp.bfloat16
HIGHEST = lax.Precision.HIGHEST
CONTRACT_LAST = (((1,), (1,)), ((), ()))


def _params(*sem):
    return pltpu.CompilerParams(dimension_semantics=sem, vmem_limit_bytes=VMEM_LIMIT)


def _rms(x, g):
    return x * lax.rsqrt(jnp.mean(x * x, axis=-1, keepdims=True) + NORM_EPS) * g


def _rope(x, cos, sin_up, sin_dn):
    reps = x.shape[1] // LANES
    c = jnp.tile(cos, (1, reps))
    su = jnp.tile(sin_up, (1, reps))
    sd = jnp.tile(sin_dn, (1, reps))
    half = ROT_DIM // 2
    return x * c + pltpu.roll(x, half, 1) * su + pltpu.roll(x, x.shape[1] - half, 1) * sd


def _rope_tables(pos):
    half = ROT_DIM // 2
    inv_freq = jnp.power(jnp.asarray(ROPE_THETA, F32), -jnp.arange(0, ROT_DIM, 2, dtype=F32) / ROT_DIM)
    ang = pos.astype(F32)[:, None] * inv_freq[None, :]
    cos, sin = jnp.cos(ang), jnp.sin(ang)
    rows = pos.shape[0]
    ones = jnp.ones((rows, HEAD_DIM - ROT_DIM), F32)
    zeros = jnp.zeros((rows, HEAD_DIM - ROT_DIM), F32)
    zh = jnp.zeros((rows, half), F32)
    cos_h = jnp.concatenate([cos, cos, ones], axis=1)
    up_h = jnp.concatenate([zh, sin, zeros], axis=1)
    dn_h = jnp.concatenate([-sin, zh, zeros], axis=1)
    reps = LANES // HEAD_DIM
    return jnp.tile(cos_h, (1, reps)), jnp.tile(up_h, (1, reps)), jnp.tile(dn_h, (1, reps))


QKV_ROWS = 256


def _qkv_prompt_kernel(x_ref, g_ref, w_ref, cos_ref, su_ref, sd_ref,
                       qt_ref, k2_ref, vt_ref, kp_ref, vp_ref, km_ref):
    xn = _rms(x_ref[0], g_ref[...]).astype(BF16)
    acc = jnp.dot(xn, w_ref[...], preferred_element_type=F32)
    cos, su, sd = cos_ref[...], su_ref[...], sd_ref[...]
    q = _rope(acc[:, :D_MODEL], cos, su, sd)
    k = _rope(acc[:, D_MODEL:2 * D_MODEL], cos, su, sd)
    v = acc[:, 2 * D_MODEL:]
    qt, kt, vt = q.T, k.T, v.T
    qt_ref[0, :, 0] = qt.reshape(N_PAIRS, LANES, QKV_ROWS)
    vt_ref[0, :, 0] = vt.reshape(N_PAIRS, LANES, QKV_ROWS).astype(BF16)
    for pr in range(N_PAIRS):
        k2_ref[0, pr, 0] = k[:, pr * LANES:(pr + 1) * LANES].astype(BF16)
    kth = kt.reshape(N_HEADS, HEAD_DIM, QKV_ROWS)
    vth = vt.reshape(N_HEADS, HEAD_DIM, QKV_ROWS)
    for p in range(QKV_ROWS // PAGE_SIZE):
        rows = slice(p * PAGE_SIZE, (p + 1) * PAGE_SIZE)
        kp_ref[0, p] = kth[:, :, rows]
        vp_ref[0, p] = vth[:, :, rows]
    km_ref[0, 0] = jnp.sum(k, axis=0, keepdims=True) * (1.0 / MOBA_BLOCK)


def _qkv_prompt(x, g, w_bf16):
    b, s, d = x.shape
    nt = s // QKV_ROWS
    cos, su, sd = _rope_tables(jnp.arange(s, dtype=jnp.int32))
    tab_spec = pl.BlockSpec((QKV_ROWS, LANES), lambda bi, si: (si, 0))
    assert QKV_ROWS == MOBA_BLOCK
    rows_spec = pl.BlockSpec((1, N_PAIRS, 1, QKV_ROWS, LANES), lambda bi, si: (bi, 0, si, 0, 0))
    cols_spec = pl.BlockSpec((1, N_PAIRS, 1, LANES, QKV_ROWS), lambda bi, si: (bi, 0, si, 0, 0))
    page_spec = pl.BlockSpec((1, QKV_ROWS // PAGE_SIZE, N_HEADS, HEAD_DIM, PAGE_SIZE),
                             lambda bi, si: (bi, si, 0, 0, 0))
    return pl.pallas_call(
        _qkv_prompt_kernel,
        grid=(b, nt),
        in_specs=[
            pl.BlockSpec((1, QKV_ROWS, d), lambda bi, si: (bi, si, 0)),
            pl.BlockSpec((1, d), lambda bi, si: (0, 0)),
            pl.BlockSpec((d, 3 * d), lambda bi, si: (0, 0)),
            tab_spec, tab_spec, tab_spec,
        ],
        out_specs=[
            cols_spec, rows_spec, cols_spec, page_spec, page_spec,
            pl.BlockSpec((1, 1, 1, d), lambda bi, si: (bi, si, 0, 0)),
        ],
        out_shape=[
            jax.ShapeDtypeStruct((b, N_PAIRS, nt, LANES, QKV_ROWS), F32),
            jax.ShapeDtypeStruct((b, N_PAIRS, nt, QKV_ROWS, LANES), BF16),
            jax.ShapeDtypeStruct((b, N_PAIRS, nt, LANES, QKV_ROWS), BF16),
            jax.ShapeDtypeStruct((b, s // PAGE_SIZE, N_HEADS, HEAD_DIM, PAGE_SIZE), F32),
            jax.ShapeDtypeStruct((b, s // PAGE_SIZE, N_HEADS, HEAD_DIM, PAGE_SIZE), F32),
            jax.ShapeDtypeStruct((b, nt, 1, d), F32),
        ],
        compiler_params=_params("parallel", "parallel"),
        name="qkv_prompt",
    )(x, g.reshape(1, d), w_bf16, cos, su, sd)


LOG2_E = 1.4426950408889634
ATTN_GROUP = 4


def _topk_block_bias(gate, blk, valid):
    n_blk = gate.shape[0]
    g = jnp.where(valid, gate, -jnp.inf)
    sel = jnp.zeros(gate.shape, jnp.bool_)
    for _ in range(MOBA_TOPK):
        m = jnp.max(g, axis=0, keepdims=True)
        hit = jnp.logical_and(g == m, m > -jnp.inf)
        idx = jnp.min(jnp.where(hit, blk, float(n_blk)), axis=0, keepdims=True)
        pick = blk == idx
        sel = jnp.logical_or(sel, pick)
        g = jnp.where(pick, -jnp.inf, g)
    return jnp.where(sel, 0.0, NEG_BIG)


def _moba_prompt_kernel(qt_ref, k_ref, vt_ref, km_ref, o_ref, bias_scr):
    qi = pl.program_id(2)
    tq = qt_ref.shape[4]
    n_blk = km_ref.shape[2]
    qt = qt_ref[0, 0, 0]
    is_a = lax.broadcasted_iota(jnp.int32, (LANES, 1), 0) < HEAD_DIM
    qta = jnp.where(is_a, qt, 0.0)
    qtb = jnp.where(is_a, 0.0, qt)
    km = km_ref[0, 0]
    blk = lax.broadcasted_iota(jnp.int32, (n_blk, tq), 0)
    valid = blk < qi
    blk = blk.astype(F32)
    gate_a = jnp.dot(km, qta, precision=HIGHEST, preferred_element_type=F32)
    gate_b = jnp.dot(km, qtb, precision=HIGHEST, preferred_element_type=F32)
    bias_scr[0] = _topk_block_bias(gate_a, blk, valid)
    bias_scr[1] = _topk_block_bias(gate_b, blk, valid)
    scale = HEAD_DIM ** -0.5 * LOG2_E
    qsa = (qta * scale).astype(BF16)
    qsb = (qtb * scale).astype(BF16)

    def head_scores(qs, blocks, adds):
        return [jnp.dot(k_ref[0, 0, j], qs, preferred_element_type=F32) + add for j, add in zip(blocks, adds)]

    def head_update(s, rows, blocks, m, l, acc):
        m_new = m
        for sj in s:
            m_new = jnp.maximum(m_new, jnp.max(sj, axis=0, keepdims=True))
        alpha = jnp.exp2(m - m_new)
        l = alpha * l
        acc = alpha * acc
        for j, sj in zip(blocks, s):
            p = jnp.exp2(sj - m_new)
            l = l + jnp.sum(p, axis=0, keepdims=True)
            acc = acc + jnp.dot(vt_ref[0, 0, j, rows, :], p.astype(BF16), preferred_element_type=F32)
        return m_new, l, acc

    def attend(blocks, adds_a, adds_b, state):
        ma, la, mb, lb, acc_a, acc_b = state
        sa = head_scores(qsa, blocks, adds_a)
        sb = head_scores(qsb, blocks, adds_b)
        ma, la, acc_a = head_update(sa, slice(0, HEAD_DIM), blocks, ma, la, acc_a)
        mb, lb, acc_b = head_update(sb, slice(HEAD_DIM, LANES), blocks, mb, lb, acc_b)
        return ma, la, mb, lb, acc_a, acc_b

    def attend_past(blocks, state):
        return attend(blocks, [bias_scr[0, pl.ds(j, 1), :] for j in blocks],
                      [bias_scr[1, pl.ds(j, 1), :] for j in blocks], state)

    key = lax.broadcasted_iota(jnp.int32, (tq, tq), 0)
    qry = lax.broadcasted_iota(jnp.int32, (tq, tq), 1)
    causal = jnp.where(key <= qry, 0.0, NEG_BIG)
    row = lambda v: jnp.full((1, tq), v, F32)
    zeros = jnp.zeros((HEAD_DIM, tq), F32)
    state = attend([qi], [causal], [causal], (row(NEG_BIG), row(0.0), row(NEG_BIG), row(0.0), zeros, zeros))

    n_groups = qi // ATTN_GROUP
    state = lax.fori_loop(
        0, n_groups, lambda g, st: attend_past([g * ATTN_GROUP + u for u in range(ATTN_GROUP)], st), state)
    ma, la, mb, lb, acc_a, acc_b = lax.fori_loop(
        n_groups * ATTN_GROUP, qi, lambda j, st: attend_past([j], st), state)
    out_t = jnp.concatenate([acc_a / la, acc_b / lb], axis=0)
    o_ref[0] = out_t.T.astype(o_ref.dtype)


def _moba_prompt(qt, k2, vt, kmean):
    b, n_pairs, n_blk, _, tq = qt.shape
    kv_map = lambda bi, pi, qi: (bi, pi, 0, 0, 0)
    return pl.pallas_call(
        _moba_prompt_kernel,
        grid=(b, n_pairs, n_blk),
        in_specs=[
            pl.BlockSpec((1, 1, 1, LANES, tq), lambda bi, pi, qi: (bi, pi, qi, 0, 0)),
            pl.BlockSpec((1, 1, n_blk, tq, LANES), kv_map),
            pl.BlockSpec((1, 1, n_blk, LANES, tq), kv_map),
            pl.BlockSpec((1, 1, n_blk, LANES), lambda bi, pi, qi: (bi, pi, 0, 0)),
        ],
        out_specs=pl.BlockSpec((1, tq, LANES), lambda bi, pi, qi: (bi, qi, pi)),
        out_shape=jax.ShapeDtypeStruct((b, n_blk * tq, n_pairs * LANES), BF16),
        scratch_shapes=[pltpu.VMEM((2, n_blk, tq), F32)],
        compiler_params=_params("parallel", "parallel", "parallel"),
        name="moba_prompt",
    )(qt, k2, vt, kmean)


def _linear_residual_kernel(x_ref, w_ref, r_ref, o_ref):
    o_ref[...] = r_ref[...] + jnp.dot(x_ref[...], w_ref[...], preferred_element_type=F32)


def _linear_residual(x_bf16, w_bf16, resid, rows):
    n, k = x_bf16.shape
    m = w_bf16.shape[1]
    return pl.pallas_call(
        _linear_residual_kernel,
        grid=(n // rows,),
        in_specs=[
            pl.BlockSpec((rows, k), lambda i: (i, 0)),
            pl.BlockSpec((k, m), lambda i: (0, 0)),
            pl.BlockSpec((rows, m), lambda i: (i, 0)),
        ],
        out_specs=pl.BlockSpec((rows, m), lambda i: (i, 0)),
        out_shape=jax.ShapeDtypeStruct((n, m), F32),
        compiler_params=_params("parallel"),
        name="linear_residual",
    )(x_bf16, w_bf16, resid)


MOE_EXPERTS_PER_STEP = 4
ROUTER_GROUP_LANE0 = N_EXPERTS
EXPERT_GROUP_SHIFT = EXPERTS_PER_GROUP.bit_length() - 1
assert 1 << EXPERT_GROUP_SHIFT == EXPERTS_PER_GROUP


def _route(logits):
    lane = lax.broadcasted_iota(jnp.int32, logits.shape, 1)
    is_g = jnp.logical_and(lane >= ROUTER_GROUP_LANE0, lane < ROUTER_GROUP_LANE0 + N_GROUPS)
    lg = jnp.where(is_g, logits, -jnp.inf)
    mg = jnp.max(lg, axis=1, keepdims=True)
    zg = jnp.sum(jnp.exp(lg - mg), axis=1, keepdims=True)
    g_w = 1.0 / zg
    g_idx = jnp.min(jnp.where(lg == mg, lane - ROUTER_GROUP_LANE0, N_GROUPS), axis=1, keepdims=True)
    in_grp = jnp.logical_and(lane < N_EXPERTS, jnp.right_shift(lane, EXPERT_GROUP_SHIFT) == g_idx)
    le = jnp.where(in_grp, logits, -jnp.inf)
    m1 = jnp.max(le, axis=1, keepdims=True)
    ze = jnp.sum(jnp.exp(le - m1), axis=1, keepdims=True)
    idx1 = jnp.min(jnp.where(le == m1, lane, LANES), axis=1, keepdims=True)
    le2 = jnp.where(lane == idx1, -jnp.inf, le)
    m2 = jnp.max(le2, axis=1, keepdims=True)
    idx2 = jnp.min(jnp.where(le2 == m2, lane, LANES), axis=1, keepdims=True)
    w1 = 1.0 / ze
    w2 = jnp.exp(m2 - m1) / ze
    c1 = g_w * w1 / (w1 + w2)
    c2 = g_w * w2 / (w1 + w2)
    return jnp.where(lane == idx1, c1, 0.0) + jnp.where(lane == idx2, c2, 0.0)


def _moe_kernel(h_ref, g_ref, wr_ref, br_ref, wg_ref, wu_ref, wd_ref, gf_ref, o_ref,
                xn_scr, gate_scr, acc_scr, *, final_norm):
    e = pl.program_id(1)

    @pl.when(e == 0)
    def _():
        h = h_ref[...]
        xn = _rms(h, g_ref[...])
        xn_scr[...] = xn.astype(BF16)
        logits = jnp.dot(xn, wr_ref[...], precision=HIGHEST, preferred_element_type=F32) + br_ref[...]
        gate_scr[...] = _route(logits)
        acc_scr[...] = h

    xn = xn_scr[...]
    gates = gate_scr[...]
    lane = lax.broadcasted_iota(jnp.int32, gates.shape, 1)
    acc = acc_scr[...]
    for c in range(MOE_EXPERTS_PER_STEP):
        eid = e * MOE_EXPERTS_PER_STEP + c
        gcol = jnp.sum(jnp.where(lane == eid, gates, 0.0), axis=1, keepdims=True)
        hg = jnp.dot(xn, wg_ref[c], preferred_element_type=F32)
        hu = jnp.dot(xn, wu_ref[c], preferred_element_type=F32)
        hdn = (hg * jax.nn.sigmoid(hg)) * hu * gcol
        acc = acc + jnp.dot(hdn.astype(BF16), wd_ref[c], preferred_element_type=F32)
    acc_scr[...] = acc

    @pl.when(e == pl.num_programs(1) - 1)
    def _():
        if final_norm:
            o_ref[...] = _rms(acc, gf_ref[...])
        else:
            o_ref[...] = acc


def _moe(h, g, w_router, b_router, wg, wu, wd, g_final, rows, final_norm):
    n, d = h.shape
    steps = N_EXPERTS // MOE_EXPERTS_PER_STEP
    row_spec = pl.BlockSpec((rows, d), lambda i, e: (i, 0))
    vec_spec = pl.BlockSpec((1, d), lambda i, e: (0, 0))
    return pl.pallas_call(
        functools.partial(_moe_kernel, final_norm=final_norm),
        grid=(n // rows, steps),
        in_specs=[
            row_spec, vec_spec,
            pl.BlockSpec((d, LANES), lambda i, e: (0, 0)),
            pl.BlockSpec((1, LANES), lambda i, e: (0, 0)),
            pl.BlockSpec((MOE_EXPERTS_PER_STEP, d, D_EXPERT), lambda i, e: (e, 0, 0)),
            pl.BlockSpec((MOE_EXPERTS_PER_STEP, d, D_EXPERT), lambda i, e: (e, 0, 0)),
            pl.BlockSpec((MOE_EXPERTS_PER_STEP, D_EXPERT, d), lambda i, e: (e, 0, 0)),
            vec_spec,
        ],
        out_specs=row_spec,
        out_shape=jax.ShapeDtypeStruct((n, d), F32),
        scratch_shapes=[
            pltpu.VMEM((rows, d), BF16),
            pltpu.VMEM((rows, LANES), F32),
            pltpu.VMEM((rows, d), F32),
        ],
        compiler_params=_params("parallel", "arbitrary"),
        name="hier_moe",
    )(h, g.reshape(1, d), w_router, b_router, wg, wu, wd, g_final.reshape(1, d))


def _router_params(w_gr, b_gr, w_er, b_er):
    d = w_gr.shape[0]
    pad = LANES - N_EXPERTS - N_GROUPS
    w = jnp.concatenate([w_er, w_gr, jnp.zeros((d, pad), F32)], axis=1)
    b = jnp.concatenate([b_er, b_gr, jnp.zeros((pad,), F32)]).reshape(1, LANES)
    return w, b


CONV_ROWS = 512
CONV_HALO = 32
CONV_ROW_CHUNK = 64
CONV_LANE_CHUNK = 256


def _layer_norm_silu(y, g, b):
    mu = jnp.mean(y, axis=-1, keepdims=True)
    yc = y - mu
    var = jnp.mean(yc * yc, axis=-1, keepdims=True)
    z = yc * lax.rsqrt(var + NORM_EPS) * g + b
    return z * jax.nn.sigmoid(z)


def _conv_prompt_kernel(x_ref, g_ref, w1_ref, b1_ref, wdw_ref, bdw_ref, lng_ref, lnb_ref, w2_ref, b2_ref,
                        o_ref, cb_ref, u_scr, y_scr, shift_scr):
    si = pl.program_id(1)
    rows = x_ref.shape[1]

    @pl.when(si == 0)
    def _():
        u_scr[0:CONV_HALO, :] = jnp.zeros((CONV_HALO, D_MODEL), F32)

    @pl.when(si > 0)
    def _():
        u_scr[0:CONV_HALO, :] = u_scr[rows:rows + CONV_HALO, :]

    x = x_ref[0]
    xn = _rms(x, g_ref[...]).astype(BF16)
    a = jnp.dot(xn, w1_ref[...], preferred_element_type=F32) + b1_ref[...]
    u_scr[CONV_HALO:CONV_HALO + rows, :] = a[:, :D_MODEL] * jax.nn.sigmoid(a[:, D_MODEL:])

    lead = CONV_HALO - (CONV_WIDTH - 1)
    span = shift_scr.shape[1]
    for sh in range(1, SUBLANES):
        shift_scr[sh - 1] = u_scr[sh:sh + span, :]
    for r0 in range(0, rows, CONV_ROW_CHUNK):
        for c0 in range(0, D_MODEL, CONV_LANE_CHUNK):
            cols = slice(c0, c0 + CONV_LANE_CHUNK)
            y = jnp.broadcast_to(bdw_ref[:, cols], (CONV_ROW_CHUNK, CONV_LANE_CHUNK))
            for w in range(CONV_WIDTH):
                base, sh = divmod(lead + w, SUBLANES)
                start = r0 + base * SUBLANES
                if sh == 0:
                    win = u_scr[start:start + CONV_ROW_CHUNK, cols]
                else:
                    win = shift_scr[sh - 1, start:start + CONV_ROW_CHUNK, cols]
                y = y + win * wdw_ref[w:w + 1, cols]
            y_scr[r0:r0 + CONV_ROW_CHUNK, cols] = y

    z = _layer_norm_silu(y_scr[...], lng_ref[...], lnb_ref[...])
    o_ref[0] = x + jnp.dot(z.astype(BF16), w2_ref[...], preferred_element_type=F32) + b2_ref[...]

    @pl.when(si == pl.num_programs(1) - 1)
    def _():
        cb_ref[0] = u_scr[rows + lead:rows + CONV_HALO, :]


def _conv_prompt(x, g, w1, b1, wdw, bdw, lng, lnb, w2, b2):
    b, s, d = x.shape
    vec = lambda n: pl.BlockSpec((1, n), lambda bi, si: (0, 0))
    return pl.pallas_call(
        _conv_prompt_kernel,
        grid=(b, s // CONV_ROWS),
        in_specs=[
            pl.BlockSpec((1, CONV_ROWS, d), lambda bi, si: (bi, si, 0)),
            vec(d),
            pl.BlockSpec((d, 2 * d), lambda bi, si: (0, 0)),
            vec(2 * d),
            pl.BlockSpec((CONV_WIDTH, d), lambda bi, si: (0, 0)),
            vec(d), vec(d), vec(d),
            pl.BlockSpec((d, d), lambda bi, si: (0, 0)),
            vec(d),
        ],
        out_specs=[
            pl.BlockSpec((1, CONV_ROWS, d), lambda bi, si: (bi, si, 0)),
            pl.BlockSpec((1, CONV_WIDTH - 1, d), lambda bi, si: (bi, 0, 0)),
        ],
        out_shape=[
            jax.ShapeDtypeStruct((b, s, d), F32),
            jax.ShapeDtypeStruct((b, CONV_WIDTH - 1, d), F32),
        ],
        scratch_shapes=[
            pltpu.VMEM((CONV_HALO + CONV_ROWS, d), F32),
            pltpu.VMEM((CONV_ROWS, d), F32),
            pltpu.VMEM((SUBLANES - 1, CONV_HALO + CONV_ROWS - SUBLANES, d), F32),
        ],
        compiler_params=_params("parallel", "arbitrary"),
        name="conformer_conv_prompt",
    )(x, g.reshape(1, d), w1, b1.reshape(1, 2 * d), wdw, bdw.reshape(1, d), lng.reshape(1, d),
      lnb.reshape(1, d), w2, b2.reshape(1, d))


def _conv_sample_kernel(x_ref, st_ref, g_ref, w1_ref, b1_ref, wdw_ref, bdw_ref, lng_ref, lnb_ref, w2_ref, b2_ref,
                        o_ref, u_ref):
    t_new, db, d = x_ref.shape
    n_state = st_ref.shape[0]
    for t in range(t_new):
        xn = _rms(x_ref[t], g_ref[...]).astype(BF16)
        a = jnp.dot(xn, w1_ref[...], preferred_element_type=F32) + b1_ref[...]
        u_ref[t] = a[:, :D_MODEL] * jax.nn.sigmoid(a[:, D_MODEL:])
    for t in range(t_new):
        y = jnp.broadcast_to(bdw_ref[...], (db, d))
        for w in range(CONV_WIDTH):
            r = t + w
            row = st_ref[r] if r < n_state else u_ref[r - n_state]
            y = y + row * wdw_ref[w:w + 1, :]
        z = _layer_norm_silu(y, lng_ref[...], lnb_ref[...])
        o_ref[t] = x_ref[t] + jnp.dot(z.astype(BF16), w2_ref[...], preferred_element_type=F32) + b2_ref[...]


def _conv_sample(x_tb, state_tb, g, w1, b1, wdw, bdw, lng, lnb, w2, b2):
    t_new, db, d = x_tb.shape
    return pl.pallas_call(
        _conv_sample_kernel,
        out_shape=[
            jax.ShapeDtypeStruct((t_new, db, d), F32),
            jax.ShapeDtypeStruct((t_new, db, d), F32),
        ],
        compiler_params=pltpu.CompilerParams(vmem_limit_bytes=VMEM_LIMIT),
        name="conformer_conv_sample",
    )(x_tb, state_tb, g.reshape(1, d), w1, b1.reshape(1, 2 * d), wdw, bdw.reshape(1, d), lng.reshape(1, d),
      lnb.reshape(1, d), w2, b2.reshape(1, d))


def _qkv_sample_kernel(x_ref, g_ref, w_ref, cos_ref, su_ref, sd_ref, q_ref, k_ref, v_ref):
    xn = _rms(x_ref[...], g_ref[...]).astype(BF16)
    acc = jnp.dot(xn, w_ref[...], preferred_element_type=F32)
    cos, su, sd = cos_ref[...], su_ref[...], sd_ref[...]
    q_ref[...] = _rope(acc[:, :D_MODEL], cos, su, sd)
    k_ref[...] = _rope(acc[:, D_MODEL:2 * D_MODEL], cos, su, sd)
    v_ref[...] = acc[:, 2 * D_MODEL:]


def _qkv_sample(x, g, w_bf16, pos):
    n, d = x.shape
    cos, su, sd = _rope_tables(pos)
    out = jax.ShapeDtypeStruct((n, d), F32)
    return pl.pallas_call(
        _qkv_sample_kernel,
        out_shape=[out, out, out],
        compiler_params=pltpu.CompilerParams(vmem_limit_bytes=VMEM_LIMIT),
        name="qkv_sample",
    )(x, g.reshape(1, d), w_bf16, cos, su, sd)


SEL_PAGES_PER_STEP = 16
Q_PAD = 8


def _past_gate_kernel(pt_ref, q_ref, *rest):
    page_refs = rest[:SEL_PAGES_PER_STEP]
    sel_ref = rest[SEL_PAGES_PER_STEP]
    km_scr = rest[SEL_PAGES_PER_STEP + 1]
    i = pl.program_id(1)
    n_blk = pl.num_programs(1) * (SEL_PAGES_PER_STEP // PAGES_PER_BLOCK)
    lane3 = lax.broadcasted_iota(jnp.int32, km_scr.shape, 2)

    @pl.when(i == 0)
    def _():
        km_scr[...] = jnp.zeros(km_scr.shape, F32)

    km = km_scr[...]
    for c in range(SEL_PAGES_PER_STEP // PAGES_PER_BLOCK):
        tot = page_refs[c * PAGES_PER_BLOCK][0, 0]
        for p in range(1, PAGES_PER_BLOCK):
            tot = tot + page_refs[c * PAGES_PER_BLOCK + p][0, 0]
        mean = jnp.sum(tot, axis=2, keepdims=True) * (1.0 / MOBA_BLOCK)
        km = jnp.where(lane3 == i * (SEL_PAGES_PER_STEP // PAGES_PER_BLOCK) + c, mean, km)
    km_scr[...] = km

    @pl.when(i == pl.num_programs(1) - 1)
    def _():
        gates = [jnp.dot(q_ref[0, h], km[h], precision=HIGHEST, preferred_element_type=F32)
                 for h in range(N_HEADS)]
        g = jnp.concatenate(gates, axis=0)
        lane = lax.broadcasted_iota(jnp.int32, g.shape, 1)
        g = jnp.where(lane < n_blk, g, -jnp.inf)
        out = jnp.zeros(g.shape, jnp.int32)
        for r in range(MOBA_TOPK):
            m = jnp.max(g, axis=1, keepdims=True)
            idx = jnp.min(jnp.where(g == m, lane, LANES), axis=1, keepdims=True)
            out = jnp.where(lane == r, idx, out)
            g = jnp.where(lane == idx, -jnp.inf, g)
        sel_ref[0] = out


def _past_gate(page_table_flat, q_pad, cache_kt, layer):
    db, h, _, dh = q_pad.shape
    n_pages = page_table_flat.shape[0] // db
    assert n_pages // PAGES_PER_BLOCK <= LANES

    def page_map(r):
        return lambda bi, i, pt: (layer, pt[bi * n_pages + i * SEL_PAGES_PER_STEP + r], 0, 0, 0)

    page_specs = [pl.BlockSpec((1, 1, h, dh, PAGE_SIZE), page_map(r)) for r in range(SEL_PAGES_PER_STEP)]
    grid_spec = pltpu.PrefetchScalarGridSpec(
        num_scalar_prefetch=1,
        grid=(db, n_pages // SEL_PAGES_PER_STEP),
        in_specs=[pl.BlockSpec((1, h, Q_PAD, dh), lambda bi, i, pt: (bi, 0, 0, 0))] + page_specs,
        out_specs=pl.BlockSpec((1, h * Q_PAD, LANES), lambda bi, i, pt: (bi, 0, 0)),
        scratch_shapes=[pltpu.VMEM((h, dh, LANES), F32)],
    )
    return pl.pallas_call(
        _past_gate_kernel,
        grid_spec=grid_spec,
        out_shape=jax.ShapeDtypeStruct((db, h * Q_PAD, LANES), jnp.int32),
        compiler_params=_params("parallel", "arbitrary"),
        name="past_gate",
    )(page_table_flat, q_pad, *([cache_kt] * SEL_PAGES_PER_STEP))


def _moba_sample_kernel(pt_ref, sel_ref, qt_ref, knt_ref, vnt_ref, ck_hbm, cv_hbm, o_ref, kbuf, vbuf, sems,
                        *, layer, t_new, n_pages):
    i = pl.program_id(0)
    n_steps = pl.num_programs(0)

    def page_copies(step, slot):
        bi = step // N_HEADS
        hi = step % N_HEADS
        copies = []
        for t in range(t_new):
            for n in range(MOBA_TOPK):
                blk = sel_ref[((bi * N_HEADS + hi) * Q_PAD + t) * MOBA_TOPK + n]
                for half in range(PAGES_PER_BLOCK):
                    r = (t * MOBA_TOPK + n) * PAGES_PER_BLOCK + half
                    page = pt_ref[bi * n_pages + blk * PAGES_PER_BLOCK + half]
                    copies.append(pltpu.make_async_copy(ck_hbm.at[layer, page, hi], kbuf.at[slot, r], sems.at[0, slot]))
                    copies.append(pltpu.make_async_copy(cv_hbm.at[layer, page, hi], vbuf.at[slot, r], sems.at[1, slot]))
        return copies

    slot = i % 2

    @pl.when(i == 0)
    def _():
        for c in page_copies(i, slot):
            c.start()

    @pl.when(i + 1 < n_steps)
    def _():
        for c in page_copies(i + 1, 1 - slot):
            c.start()

    for c in page_copies(i, slot):
        c.wait()

    qt = qt_ref[0, 0] * HEAD_DIM ** -0.5
    knt = knt_ref[0, 0]
    vnt = vnt_ref[0, 0]
    lane_new = lax.broadcasted_iota(jnp.int32, (1, Q_PAD), 1)
    lane_out = lax.broadcasted_iota(jnp.int32, (HEAD_DIM, Q_PAD), 1)
    out_t = jnp.zeros((HEAD_DIM, Q_PAD), F32)
    pages_per_token = MOBA_TOPK * PAGES_PER_BLOCK
    for t in range(t_new):
        q_col = qt[:, t:t + 1]
        s_new = jnp.sum(knt * q_col, axis=0, keepdims=True)
        s_new = jnp.where(lane_new <= t, s_new, NEG_BIG)
        s_sel = [jnp.sum(kbuf[slot, t * pages_per_token + r] * q_col, axis=0, keepdims=True)
                 for r in range(pages_per_token)]
        m_sel = s_sel[0]
        for s in s_sel[1:]:
            m_sel = jnp.maximum(m_sel, s)
        m = jnp.maximum(jnp.max(m_sel, axis=1, keepdims=True), jnp.max(s_new, axis=1, keepdims=True))
        p_new = jnp.exp(s_new - m)
        l = jnp.sum(p_new, axis=1, keepdims=True)
        acc = None
        p_sum = None
        for r, s in enumerate(s_sel):
            p = jnp.exp(s - m)
            p_sum = p if p_sum is None else p_sum + p
            term = vbuf[slot, t * pages_per_token + r] * p
            acc = term if acc is None else acc + term
        l = l + jnp.sum(p_sum, axis=1, keepdims=True)
        o_col = jnp.sum(acc, axis=1, keepdims=True) + jnp.sum(vnt * p_new, axis=1, keepdims=True)
        out_t = jnp.where(lane_out == t, o_col / l, out_t)
    o_ref[0, 0] = out_t


def _moba_sample(page_table_flat, sel_flat, qt_pad, knt_pad, vnt_pad, cache_kt, cache_vt, layer, t_new):
    db, h, dh, _ = qt_pad.shape
    n_pages = page_table_flat.shape[0] // db
    n_buf = t_new * MOBA_TOPK * PAGES_PER_BLOCK
    new_spec = pl.BlockSpec((1, 1, dh, Q_PAD), lambda i, pt, sel: (i // h, i % h, 0, 0))
    hbm_spec = pl.BlockSpec(memory_space=pl.ANY)
    grid_spec = pltpu.PrefetchScalarGridSpec(
        num_scalar_prefetch=2,
        grid=(db * h,),
        in_specs=[new_spec, new_spec, new_spec, hbm_spec, hbm_spec],
        out_specs=new_spec,
        scratch_shapes=[
            pltpu.VMEM((2, n_buf, dh, PAGE_SIZE), F32),
            pltpu.VMEM((2, n_buf, dh, PAGE_SIZE), F32),
            pltpu.SemaphoreType.DMA((2, 2)),
        ],
    )
    return pl.pallas_call(
        functools.partial(_moba_sample_kernel, layer=layer, t_new=t_new, n_pages=n_pages),
        grid_spec=grid_spec,
        out_shape=jax.ShapeDtypeStruct((db, h, dh, Q_PAD), F32),
        compiler_params=_params("arbitrary"),
        name="moba_sample",
    )(page_table_flat, sel_flat, qt_pad, knt_pad, vnt_pad, cache_kt, cache_vt)


def _pad_cols(x_bhdt):
    db, h, dh, t_new = x_bhdt.shape
    return jnp.concatenate([x_bhdt, jnp.zeros((db, h, dh, Q_PAD - t_new), x_bhdt.dtype)], axis=3)


def _pad_rows(x_bhtd):
    db, h, t_new, dh = x_bhtd.shape
    return jnp.concatenate([x_bhtd, jnp.zeros((db, h, Q_PAD - t_new, dh), x_bhtd.dtype)], axis=2)


PROMPT_ROWS = 512
MOE_PROMPT_ROWS = 1024


def kernel(x_prompt, x_sample, cache_k, cache_v, page_table, state_conv, norm_mix, norm_ffn, norm_final,
           w_qkv, w_o, w_pw1, b_pw1, w_dw, b_dw, conv_ln_g, conv_ln_b, w_pw2, b_pw2,
           w_group_router, b_group_router, w_expert_router, b_expert_router,
           w_exp_gate, w_exp_up, w_exp_down):
    b, s, d = x_prompt.shape
    db, t_new, _ = x_sample.shape
    assert page_table.shape[1] % PAGES_PER_BLOCK == 0
    past_len = page_table.shape[1] * PAGE_SIZE
    n_p = b * s
    n_s = db * t_new

    def moe(h, layer, rows, final):
        w_r, b_r = _router_params(w_group_router[layer], b_group_router[layer],
                                  w_expert_router[layer], b_expert_router[layer])
        return _moe(h, norm_ffn[layer], w_r, b_r, w_exp_gate[layer].astype(BF16), w_exp_up[layer].astype(BF16),
                    w_exp_down[layer].astype(BF16), norm_final, rows, final)

    wqkv = w_qkv[0].astype(BF16)
    wo = w_o[0].astype(BF16)
    qt, k2, vt, kt_paged, vt_paged, kmean = _qkv_prompt(x_prompt, norm_mix[0], wqkv)
    n_blk = s // MOBA_BLOCK
    kmean = kmean.reshape(b, n_blk, N_PAIRS, LANES).transpose(0, 2, 1, 3)
    attn_p = _moba_prompt(qt, k2, vt, kmean)
    hp = _linear_residual(attn_p.reshape(n_p, d), wo, x_prompt.reshape(n_p, d), PROMPT_ROWS)

    xs = x_sample.reshape(n_s, d)
    pos_s = past_len + jnp.tile(jnp.arange(t_new, dtype=jnp.int32), db)
    qs, ks, vs = _qkv_sample(xs, norm_mix[0], wqkv, pos_s)
    q_bthd = qs.reshape(db, t_new, N_HEADS, HEAD_DIM)
    k_bthd = ks.reshape(db, t_new, N_HEADS, HEAD_DIM)
    v_bthd = vs.reshape(db, t_new, N_HEADS, HEAD_DIM)
    k_bhtd = k_bthd.transpose(0, 2, 1, 3)
    v_bhtd = v_bthd.transpose(0, 2, 1, 3)
    pt_flat = page_table.reshape(-1)
    cache_kt = jnp.swapaxes(cache_k, 3, 4)
    cache_vt = jnp.swapaxes(cache_v, 3, 4)
    sel = _past_gate(pt_flat, _pad_rows(q_bthd.transpose(0, 2, 1, 3)), cache_kt, 0)[:, :, :MOBA_TOPK]
    to_t = lambda x: _pad_cols(x.transpose(0, 2, 3, 1))
    attn_s = _moba_sample(pt_flat, sel.reshape(-1), to_t(q_bthd), to_t(k_bthd), to_t(v_bthd),
                          cache_kt, cache_vt, 0, t_new)
    attn_s = attn_s[:, :, :, :t_new].transpose(0, 3, 1, 2).reshape(n_s, d).astype(BF16)
    hs = _linear_residual(attn_s, wo, xs, n_s)

    hp = moe(hp, 0, MOE_PROMPT_ROWS, False)
    hs = moe(hs, 0, n_s, False)

    w1 = w_pw1[0].astype(BF16)
    w2 = w_pw2[0].astype(BF16)
    conv_w = (norm_mix[1], w1, b_pw1[0], w_dw[0], b_dw[0], conv_ln_g[0], conv_ln_b[0], w2, b_pw2[0])
    hp, conv_p = _conv_prompt(hp.reshape(b, s, d), *conv_w)
    hs_tb, u_tb = _conv_sample(hs.reshape(db, t_new, d).transpose(1, 0, 2), state_conv[0].transpose(1, 0, 2), *conv_w)
    hs = hs_tb.transpose(1, 0, 2).reshape(n_s, d)
    conv_s = jnp.concatenate([state_conv[0], u_tb.transpose(1, 0, 2)], axis=1)[:, t_new:]

    y_prompt = moe(hp.reshape(n_p, d), 1, MOE_PROMPT_ROWS, True).reshape(b, s, d)
    y_sample = moe(hs, 1, n_s, True).reshape(db, t_new, d)

    return (y_prompt, y_sample, jnp.swapaxes(kt_paged, 3, 4)[None], jnp.swapaxes(vt_paged, 3, 4)[None],
            k_bhtd[None], v_bhtd[None], conv_p[None], conv_s[None])
```

```python
import functools

import jax
import jax.numpy as jnp
from jax import lax
from jax.experimental import pallas as pl
from jax.experimental.pallas import tpu as pltpu

D_MODEL = 1024
N_HEADS = 16
HEAD_DIM = D_MODEL // N_HEADS
N_PAIRS = N_HEADS // 2
ROT_DIM = HEAD_DIM // 4
ROPE_THETA = 500000.0
MOBA_BLOCK = 256
MOBA_TOPK = 3
PAGE_SIZE = 128
PAGES_PER_BLOCK = MOBA_BLOCK // PAGE_SIZE
CONV_WIDTH = 31
N_GROUPS = 4
EXPERTS_PER_GROUP = 8
N_EXPERTS = N_GROUPS * EXPERTS_PER_GROUP
D_EXPERT = 256
NORM_EPS = 1e-6
LANES = 128
SUBLANES = 8
NEG_BIG = -1e30
VMEM_LIMIT = 56 * 1024 * 1024
F32 = jnp.float32
BF16 = jnp.bfloat16
HIGHEST = lax.Precision.HIGHEST
CONTRACT_LAST = (((1,), (1,)), ((), ()))


def _params(*sem):
    return pltpu.CompilerParams(dimension_semantics=sem, vmem_limit_bytes=VMEM_LIMIT)


def _rms(x, g):
    return x * lax.rsqrt(jnp.mean(x * x, axis=-1, keepdims=True) + NORM_EPS) * g


def _rope(x, cos, sin_up, sin_dn):
    reps = x.shape[1] // LANES
    c = jnp.tile(cos, (1, reps))
    su = jnp.tile(sin_up, (1, reps))
    sd = jnp.tile(sin_dn, (1, reps))
    half = ROT_DIM // 2
    return x * c + pltpu.roll(x, half, 1) * su + pltpu.roll(x, x.shape[1] - half, 1) * sd


def _rope_tables(pos):
    half = ROT_DIM // 2
    inv_freq = jnp.power(jnp.asarray(ROPE_THETA, F32), -jnp.arange(0, ROT_DIM, 2, dtype=F32) / ROT_DIM)
    ang = pos.astype(F32)[:, None] * inv_freq[None, :]
    cos, sin = jnp.cos(ang), jnp.sin(ang)
    rows = pos.shape[0]
    ones = jnp.ones((rows, HEAD_DIM - ROT_DIM), F32)
    zeros = jnp.zeros((rows, HEAD_DIM - ROT_DIM), F32)
    zh = jnp.zeros((rows, half), F32)
    cos_h = jnp.concatenate([cos, cos, ones], axis=1)
    up_h = jnp.concatenate([zh, sin, zeros], axis=1)
    dn_h = jnp.concatenate([-sin, zh, zeros], axis=1)
    reps = LANES // HEAD_DIM
    return jnp.tile(cos_h, (1, reps)), jnp.tile(up_h, (1, reps)), jnp.tile(dn_h, (1, reps))


QKV_ROWS = 256
V_AUG_ROWS = HEAD_DIM + 16


def _qkv_prompt_kernel(x_ref, g_ref, w_ref, cos_ref, su_ref, sd_ref,
                       qt_ref, k2_ref, vt_ref, kp_ref, vp_ref, km_ref):
    xn = _rms(x_ref[0], g_ref[...]).astype(BF16)
    acc = jnp.dot(xn, w_ref[...], preferred_element_type=F32)
    cos, su, sd = cos_ref[...], su_ref[...], sd_ref[...]
    q = _rope(acc[:, :D_MODEL], cos, su, sd)
    k = _rope(acc[:, D_MODEL:2 * D_MODEL], cos, su, sd)
    v = acc[:, 2 * D_MODEL:]
    qt, kt, vt = q.T, k.T, v.T
    qt_ref[0, :, 0] = qt.reshape(N_PAIRS, LANES, QKV_ROWS)
    kth = kt.reshape(N_HEADS, HEAD_DIM, QKV_ROWS)
    vth = vt.reshape(N_HEADS, HEAD_DIM, QKV_ROWS)
    lane = lax.broadcasted_iota(jnp.int32, (QKV_ROWS, LANES), 1)
    block_onehot = jnp.where(lane == HEAD_DIM + pl.program_id(1), 1.0, 0.0)
    low = lane < HEAD_DIM
    for pr in range(N_PAIRS):
        k_pair = k[:, pr * LANES:(pr + 1) * LANES]
        k2_ref[0, 2 * pr, 0] = jnp.where(low, k_pair, block_onehot).astype(BF16)
        k2_ref[0, 2 * pr + 1, 0] = jnp.where(low, pltpu.roll(k_pair, HEAD_DIM, 1), block_onehot).astype(BF16)
    ones = jnp.ones((V_AUG_ROWS - HEAD_DIM, QKV_ROWS), F32)
    for h in range(N_HEADS):
        vt_ref[0, h, 0] = jnp.concatenate([vth[h], ones], axis=0).astype(BF16)
    for p in range(QKV_ROWS // PAGE_SIZE):
        rows = slice(p * PAGE_SIZE, (p + 1) * PAGE_SIZE)
        kp_ref[0, p] = kth[:, :, rows]
        vp_ref[0, p] = vth[:, :, rows]
    km_ref[0, 0] = jnp.sum(k, axis=0, keepdims=True) * (1.0 / MOBA_BLOCK)


def _qkv_prompt(x, g, w_bf16):
    b, s, d = x.shape
    nt = s // QKV_ROWS
    cos, su, sd = _rope_tables(jnp.arange(s, dtype=jnp.int32))
    tab_spec = pl.BlockSpec((QKV_ROWS, LANES), lambda bi, si: (si, 0))
    assert QKV_ROWS == MOBA_BLOCK and HEAD_DIM + nt <= LANES
    head_map = lambda bi, si: (bi, 0, si, 0, 0)
    qt_spec = pl.BlockSpec((1, N_PAIRS, 1, LANES, QKV_ROWS), head_map)
    k_spec = pl.BlockSpec((1, N_HEADS, 1, QKV_ROWS, LANES), head_map)
    vt_spec = pl.BlockSpec((1, N_HEADS, 1, V_AUG_ROWS, QKV_ROWS), head_map)
    page_spec = pl.BlockSpec((1, QKV_ROWS // PAGE_SIZE, N_HEADS, HEAD_DIM, PAGE_SIZE),
                             lambda bi, si: (bi, si, 0, 0, 0))
    return pl.pallas_call(
        _qkv_prompt_kernel,
        grid=(b, nt),
        in_specs=[
            pl.BlockSpec((1, QKV_ROWS, d), lambda bi, si: (bi, si, 0)),
            pl.BlockSpec((1, d), lambda bi, si: (0, 0)),
            pl.BlockSpec((d, 3 * d), lambda bi, si: (0, 0)),
            tab_spec, tab_spec, tab_spec,
        ],
        out_specs=[
            qt_spec, k_spec, vt_spec, page_spec, page_spec,
            pl.BlockSpec((1, 1, 1, d), lambda bi, si: (bi, si, 0, 0)),
        ],
        out_shape=[
            jax.ShapeDtypeStruct((b, N_PAIRS, nt, LANES, QKV_ROWS), F32),
            jax.ShapeDtypeStruct((b, N_HEADS, nt, QKV_ROWS, LANES), BF16),
            jax.ShapeDtypeStruct((b, N_HEADS, nt, V_AUG_ROWS, QKV_ROWS), BF16),
            jax.ShapeDtypeStruct((b, s // PAGE_SIZE, N_HEADS, HEAD_DIM, PAGE_SIZE), F32),
            jax.ShapeDtypeStruct((b, s // PAGE_SIZE, N_HEADS, HEAD_DIM, PAGE_SIZE), F32),
            jax.ShapeDtypeStruct((b, nt, 1, d), F32),
        ],
        compiler_params=_params("parallel", "parallel"),
        name="qkv_prompt",
    )(x, g.reshape(1, d), w_bf16, cos, su, sd)


LOG2_E = 1.4426950408889634
ATTN_GROUPS = (8, 4, 2, 1)


def _topk_block_bias(gate, blk, valid):
    n_blk = gate.shape[0]
    g = jnp.where(valid, gate, -jnp.inf)
    sel = jnp.zeros(gate.shape, jnp.bool_)
    for _ in range(MOBA_TOPK):
        m = jnp.max(g, axis=0, keepdims=True)
        hit = jnp.logical_and(g == m, m > -jnp.inf)
        idx = jnp.min(jnp.where(hit, blk, float(n_blk)), axis=0, keepdims=True)
        pick = blk == idx
        sel = jnp.logical_or(sel, pick)
        g = jnp.where(pick, -jnp.inf, g)
    return jnp.where(sel, 0.0, NEG_BIG)


def _moba_prompt_kernel(qt_ref, k_ref, vt_ref, km_ref, o_ref):
    qi = pl.program_id(2)
    tq = qt_ref.shape[4]
    n_blk = km_ref.shape[2]
    qt = qt_ref[0, 0, 0]
    km = km_ref[0, 0]
    is_a = lax.broadcasted_iota(jnp.int32, (1, LANES), 1) < HEAD_DIM
    km_heads = jnp.concatenate([jnp.where(is_a, km, 0.0), jnp.where(is_a, 0.0, km)], axis=0)
    gates_ab = jnp.dot(km_heads, qt, precision=HIGHEST, preferred_element_type=F32)
    gates = (gates_ab[:n_blk], gates_ab[n_blk:])
    blk_id = lax.broadcasted_iota(jnp.int32, (n_blk, tq), 0)
    valid = blk_id < qi
    blk = blk_id.astype(F32)
    scale = HEAD_DIM ** -0.5 * LOG2_E
    pad = jnp.zeros((LANES - HEAD_DIM - n_blk, tq), F32)
    q_own, q_aug = [], []
    for hh in range(2):
        bias = _topk_block_bias(gates[hh], blk, valid)
        q_h = qt[hh * HEAD_DIM:(hh + 1) * HEAD_DIM] * scale
        q_own.append(jnp.concatenate([q_h, jnp.zeros((LANES - HEAD_DIM, tq), F32)], axis=0).astype(BF16))
        q_aug.append(jnp.concatenate([q_h, bias, pad], axis=0).astype(BF16))

    def head_scores(hh, blocks, q_side=q_aug):
        return [jnp.dot(k_ref[0, hh, j], q_side[hh], preferred_element_type=F32) for j in blocks]

    def head_update(s, hh, j, m, acc):
        m_new = jnp.maximum(m, jnp.max(s, axis=0, keepdims=True))
        p = jnp.exp2(s - m_new).astype(BF16)
        acc = jnp.exp2(m - m_new) * acc + jnp.dot(vt_ref[0, hh, j], p, preferred_element_type=F32)
        return m_new, acc

    def absorb(blocks, sa, sb, state):
        ma, acc_a, mb, acc_b = state
        for j, sja, sjb in zip(blocks, sa, sb):
            ma, acc_a = head_update(sja, 0, j, ma, acc_a)
            mb, acc_b = head_update(sjb, 1, j, mb, acc_b)
        return ma, acc_a, mb, acc_b

    def attend(blocks, mask, state):
        if mask is None:
            sa, sb = head_scores(0, blocks), head_scores(1, blocks)
        else:
            sa = [s + mask for s in head_scores(0, blocks, q_own)]
            sb = [s + mask for s in head_scores(1, blocks, q_own)]
        return absorb(blocks, sa, sb, state)

    key = lax.broadcasted_iota(jnp.int32, (tq, tq), 0)
    qry = lax.broadcasted_iota(jnp.int32, (tq, tq), 1)
    causal = jnp.where(key <= qry, 0.0, NEG_BIG)
    m0 = jnp.full((1, tq), NEG_BIG, F32)
    acc0 = jnp.zeros((V_AUG_ROWS, tq), F32)
    state = attend([qi], causal, (m0, acc0, m0, acc0))

    done = 0
    for width in ATTN_GROUPS:
        n_groups = (qi - done) // width
        state = lax.fori_loop(
            0, n_groups, lambda g, st, w=width, d=done: attend([d + g * w + u for u in range(w)], None, st), state)
        done = done + n_groups * width
    _, acc_a, _, acc_b = state
    out_t = jnp.concatenate([acc_a[:HEAD_DIM] / acc_a[HEAD_DIM:HEAD_DIM + 1],
                             acc_b[:HEAD_DIM] / acc_b[HEAD_DIM:HEAD_DIM + 1]], axis=0)
    o_ref[0] = out_t.T.astype(o_ref.dtype)


def _moba_prompt(qt, k_aug, vt_aug, kmean):
    b, n_pairs, n_blk, _, tq = qt.shape
    kv_map = lambda bi, pi, qi: (bi, pi, 0, 0, 0)
    return pl.pallas_call(
        _moba_prompt_kernel,
        grid=(b, n_pairs, n_blk),
        in_specs=[
            pl.BlockSpec((1, 1, 1, LANES, tq), lambda bi, pi, qi: (bi, pi, qi, 0, 0)),
            pl.BlockSpec((1, 2, n_blk, tq, LANES), kv_map),
            pl.BlockSpec((1, 2, n_blk, V_AUG_ROWS, tq), kv_map),
            pl.BlockSpec((1, 1, n_blk, LANES), lambda bi, pi, qi: (bi, pi, 0, 0)),
        ],
        out_specs=pl.BlockSpec((1, tq, LANES), lambda bi, pi, qi: (bi, qi, pi)),
        out_shape=jax.ShapeDtypeStruct((b, n_blk * tq, n_pairs * LANES), BF16),
        compiler_params=_params("parallel", "parallel", "parallel"),
        name="moba_prompt",
    )(qt, k_aug, vt_aug, kmean)


def _linear_residual_kernel(x_ref, w_ref, r_ref, o_ref):
    o_ref[...] = r_ref[...] + jnp.dot(x_ref[...], w_ref[...], preferred_element_type=F32)


def _linear_residual(x_bf16, w_bf16, resid, rows):
    n, k = x_bf16.shape
    m = w_bf16.shape[1]
    return pl.pallas_call(
        _linear_residual_kernel,
        grid=(n // rows,),
        in_specs=[
            pl.BlockSpec((rows, k), lambda i: (i, 0)),
            pl.BlockSpec((k, m), lambda i: (0, 0)),
            pl.BlockSpec((rows, m), lambda i: (i, 0)),
        ],
        out_specs=pl.BlockSpec((rows, m), lambda i: (i, 0)),
        out_shape=jax.ShapeDtypeStruct((n, m), F32),
        compiler_params=_params("parallel"),
        name="linear_residual",
    )(x_bf16, w_bf16, resid)


MOE_EXPERTS_PER_STEP = 4
ROUTER_GROUP_LANE0 = N_EXPERTS
EXPERT_GROUP_SHIFT = EXPERTS_PER_GROUP.bit_length() - 1
assert 1 << EXPERT_GROUP_SHIFT == EXPERTS_PER_GROUP


def _route(logits):
    lane = lax.broadcasted_iota(jnp.int32, logits.shape, 1)
    is_g = jnp.logical_and(lane >= ROUTER_GROUP_LANE0, lane < ROUTER_GROUP_LANE0 + N_GROUPS)
    lg = jnp.where(is_g, logits, -jnp.inf)
    mg = jnp.max(lg, axis=1, keepdims=True)
    zg = jnp.sum(jnp.exp(lg - mg), axis=1, keepdims=True)
    g_w = 1.0 / zg
    g_idx = jnp.min(jnp.where(lg == mg, lane - ROUTER_GROUP_LANE0, N_GROUPS), axis=1, keepdims=True)
    in_grp = jnp.logical_and(lane < N_EXPERTS, jnp.right_shift(lane, EXPERT_GROUP_SHIFT) == g_idx)
    le = jnp.where(in_grp, logits, -jnp.inf)
    m1 = jnp.max(le, axis=1, keepdims=True)
    ze = jnp.sum(jnp.exp(le - m1), axis=1, keepdims=True)
    idx1 = jnp.min(jnp.where(le == m1, lane, LANES), axis=1, keepdims=True)
    le2 = jnp.where(lane == idx1, -jnp.inf, le)
    m2 = jnp.max(le2, axis=1, keepdims=True)
    idx2 = jnp.min(jnp.where(le2 == m2, lane, LANES), axis=1, keepdims=True)
    w1 = 1.0 / ze
    w2 = jnp.exp(m2 - m1) / ze
    c1 = g_w * w1 / (w1 + w2)
    c2 = g_w * w2 / (w1 + w2)
    return jnp.where(lane == idx1, c1, 0.0) + jnp.where(lane == idx2, c2, 0.0)


def _moe_kernel(h_ref, g_ref, wr_ref, br_ref, wg_ref, wu_ref, wd_ref, gf_ref, o_ref,
                xn_scr, gate_scr, acc_scr, *, final_norm):
    e = pl.program_id(1)

    @pl.when(e == 0)
    def _():
        h = h_ref[...]
        xn = _rms(h, g_ref[...])
        xn_scr[...] = xn.astype(BF16)
        logits = jnp.dot(xn, wr_ref[...], precision=HIGHEST, preferred_element_type=F32) + br_ref[...]
        gate_scr[...] = _route(logits)
        acc_scr[...] = h

    xn = xn_scr[...]
    gates = gate_scr[...]
    lane = lax.broadcasted_iota(jnp.int32, gates.shape, 1)
    acc = acc_scr[...]
    for c in range(MOE_EXPERTS_PER_STEP):
        eid = e * MOE_EXPERTS_PER_STEP + c
        gcol = jnp.sum(jnp.where(lane == eid, gates, 0.0), axis=1, keepdims=True)
        hg = jnp.dot(xn, wg_ref[c], preferred_element_type=F32)
        hu = jnp.dot(xn, wu_ref[c], preferred_element_type=F32)
        hdn = (hg * jax.nn.sigmoid(hg)) * hu * gcol
        acc = acc + jnp.dot(hdn.astype(BF16), wd_ref[c], preferred_element_type=F32)
    acc_scr[...] = acc

    @pl.when(e == pl.num_programs(1) - 1)
    def _():
        if final_norm:
            o_ref[...] = _rms(acc, gf_ref[...])
        else:
            o_ref[...] = acc


def _moe(h, g, w_router, b_router, wg, wu, wd, g_final, rows, final_norm):
    n, d = h.shape
    steps = N_EXPERTS // MOE_EXPERTS_PER_STEP
    row_spec = pl.BlockSpec((rows, d), lambda i, e: (i, 0))
    vec_spec = pl.BlockSpec((1, d), lambda i, e: (0, 0))
    return pl.pallas_call(
        functools.partial(_moe_kernel, final_norm=final_norm),
        grid=(n // rows, steps),
        in_specs=[
            row_spec, vec_spec,
            pl.BlockSpec((d, LANES), lambda i, e: (0, 0)),
            pl.BlockSpec((1, LANES), lambda i, e: (0, 0)),
            pl.BlockSpec((MOE_EXPERTS_PER_STEP, d, D_EXPERT), lambda i, e: (e, 0, 0)),
            pl.BlockSpec((MOE_EXPERTS_PER_STEP, d, D_EXPERT), lambda i, e: (e, 0, 0)),
            pl.BlockSpec((MOE_EXPERTS_PER_STEP, D_EXPERT, d), lambda i, e: (e, 0, 0)),
            vec_spec,
        ],
        out_specs=row_spec,
        out_shape=jax.ShapeDtypeStruct((n, d), F32),
        scratch_shapes=[
            pltpu.VMEM((rows, d), BF16),
            pltpu.VMEM((rows, LANES), F32),
            pltpu.VMEM((rows, d), F32),
        ],
        compiler_params=_params("parallel", "arbitrary"),
        name="hier_moe",
    )(h, g.reshape(1, d), w_router, b_router, wg, wu, wd, g_final.reshape(1, d))


def _router_params(w_gr, b_gr, w_er, b_er):
    d = w_gr.shape[0]
    pad = LANES - N_EXPERTS - N_GROUPS
    w = jnp.concatenate([w_er, w_gr, jnp.zeros((d, pad), F32)], axis=1)
    b = jnp.concatenate([b_er, b_gr, jnp.zeros((pad,), F32)]).reshape(1, LANES)
    return w, b


CONV_ROWS = 512
CONV_HALO = 32
CONV_ROW_CHUNK = 64
CONV_LANE_CHUNK = 256


def _layer_norm_silu(y, g, b):
    mu = jnp.mean(y, axis=-1, keepdims=True)
    yc = y - mu
    var = jnp.mean(yc * yc, axis=-1, keepdims=True)
    z = yc * lax.rsqrt(var + NORM_EPS) * g + b
    return z * jax.nn.sigmoid(z)


def _conv_prompt_kernel(x_ref, g_ref, w1_ref, b1_ref, wdw_ref, bdw_ref, lng_ref, lnb_ref, w2_ref, b2_ref,
                        o_ref, cb_ref, u_scr, y_scr, shift_scr):
    si = pl.program_id(1)
    rows = x_ref.shape[1]

    @pl.when(si == 0)
    def _():
        u_scr[0:CONV_HALO, :] = jnp.zeros((CONV_HALO, D_MODEL), F32)

    @pl.when(si > 0)
    def _():
        u_scr[0:CONV_HALO, :] = u_scr[rows:rows + CONV_HALO, :]

    x = x_ref[0]
    xn = _rms(x, g_ref[...]).astype(BF16)
    a = jnp.dot(xn, w1_ref[...], preferred_element_type=F32) + b1_ref[...]
    u_scr[CONV_HALO:CONV_HALO + rows, :] = a[:, :D_MODEL] * jax.nn.sigmoid(a[:, D_MODEL:])

    lead = CONV_HALO - (CONV_WIDTH - 1)
    span = shift_scr.shape[1]
    for sh in range(1, SUBLANES):
        shift_scr[sh - 1] = u_scr[sh:sh + span, :]
    for r0 in range(0, rows, CONV_ROW_CHUNK):
        for c0 in range(0, D_MODEL, CONV_LANE_CHUNK):
            cols = slice(c0, c0 + CONV_LANE_CHUNK)
            y = jnp.broadcast_to(bdw_ref[:, cols], (CONV_ROW_CHUNK, CONV_LANE_CHUNK))
            for w in range(CONV_WIDTH):
                base, sh = divmod(lead + w, SUBLANES)
                start = r0 + base * SUBLANES
                if sh == 0:
                    win = u_scr[start:start + CONV_ROW_CHUNK, cols]
                else:
                    win = shift_scr[sh - 1, start:start + CONV_ROW_CHUNK, cols]
                y = y + win * wdw_ref[w:w + 1, cols]
            y_scr[r0:r0 + CONV_ROW_CHUNK, cols] = y

    z = _layer_norm_silu(y_scr[...], lng_ref[...], lnb_ref[...])
    o_ref[0] = x + jnp.dot(z.astype(BF16), w2_ref[...], preferred_element_type=F32) + b2_ref[...]

    @pl.when(si == pl.num_programs(1) - 1)
    def _():
        cb_ref[0] = u_scr[rows + lead:rows + CONV_HALO, :]


def _conv_prompt(x, g, w1, b1, wdw, bdw, lng, lnb, w2, b2):
    b, s, d = x.shape
    vec = lambda n: pl.BlockSpec((1, n), lambda bi, si: (0, 0))
    return pl.pallas_call(
        _conv_prompt_kernel,
        grid=(b, s // CONV_ROWS),
        in_specs=[
            pl.BlockSpec((1, CONV_ROWS, d), lambda bi, si: (bi, si, 0)),
            vec(d),
            pl.BlockSpec((d, 2 * d), lambda bi, si: (0, 0)),
            vec(2 * d),
            pl.BlockSpec((CONV_WIDTH, d), lambda bi, si: (0, 0)),
            vec(d), vec(d), vec(d),
            pl.BlockSpec((d, d), lambda bi, si: (0, 0)),
            vec(d),
        ],
        out_specs=[
            pl.BlockSpec((1, CONV_ROWS, d), lambda bi, si: (bi, si, 0)),
            pl.BlockSpec((1, CONV_WIDTH - 1, d), lambda bi, si: (bi, 0, 0)),
        ],
        out_shape=[
            jax.ShapeDtypeStruct((b, s, d), F32),
            jax.ShapeDtypeStruct((b, CONV_WIDTH - 1, d), F32),
        ],
        scratch_shapes=[
            pltpu.VMEM((CONV_HALO + CONV_ROWS, d), F32),
            pltpu.VMEM((CONV_ROWS, d), F32),
            pltpu.VMEM((SUBLANES - 1, CONV_HALO + CONV_ROWS - SUBLANES, d), F32),
        ],
        compiler_params=_params("parallel", "arbitrary"),
        name="conformer_conv_prompt",
    )(x, g.reshape(1, d), w1, b1.reshape(1, 2 * d), wdw, bdw.reshape(1, d), lng.reshape(1, d),
      lnb.reshape(1, d), w2, b2.reshape(1, d))


def _conv_sample_kernel(x_ref, st_ref, g_ref, w1_ref, b1_ref, wdw_ref, bdw_ref, lng_ref, lnb_ref, w2_ref, b2_ref,
                        o_ref, u_ref):
    t_new, db, d = x_ref.shape
    n_state = st_ref.shape[0]
    for t in range(t_new):
        xn = _rms(x_ref[t], g_ref[...]).astype(BF16)
        a = jnp.dot(xn, w1_ref[...], preferred_element_type=F32) + b1_ref[...]
        u_ref[t] = a[:, :D_MODEL] * jax.nn.sigmoid(a[:, D_MODEL:])
    for t in range(t_new):
        y = jnp.broadcast_to(bdw_ref[...], (db, d))
        for w in range(CONV_WIDTH):
            r = t + w
            row = st_ref[r] if r < n_state else u_ref[r - n_state]
            y = y + row * wdw_ref[w:w + 1, :]
        z = _layer_norm_silu(y, lng_ref[...], lnb_ref[...])
        o_ref[t] = x_ref[t] + jnp.dot(z.astype(BF16), w2_ref[...], preferred_element_type=F32) + b2_ref[...]


def _conv_sample(x_tb, state_tb, g, w1, b1, wdw, bdw, lng, lnb, w2, b2):
    t_new, db, d = x_tb.shape
    return pl.pallas_call(
        _conv_sample_kernel,
        out_shape=[
            jax.ShapeDtypeStruct((t_new, db, d), F32),
            jax.ShapeDtypeStruct((t_new, db, d), F32),
        ],
        compiler_params=pltpu.CompilerParams(vmem_limit_bytes=VMEM_LIMIT),
        name="conformer_conv_sample",
    )(x_tb, state_tb, g.reshape(1, d), w1, b1.reshape(1, 2 * d), wdw, bdw.reshape(1, d), lng.reshape(1, d),
      lnb.reshape(1, d), w2, b2.reshape(1, d))


def _qkv_sample_kernel(x_ref, g_ref, w_ref, cos_ref, su_ref, sd_ref, q_ref, k_ref, v_ref):
    xn = _rms(x_ref[...], g_ref[...]).astype(BF16)
    acc = jnp.dot(xn, w_ref[...], preferred_element_type=F32)
    cos, su, sd = cos_ref[...], su_ref[...], sd_ref[...]
    q_ref[...] = _rope(acc[:, :D_MODEL], cos, su, sd)
    k_ref[...] = _rope(acc[:, D_MODEL:2 * D_MODEL], cos, su, sd)
    v_ref[...] = acc[:, 2 * D_MODEL:]


def _qkv_sample(x, g, w_bf16, pos):
    n, d = x.shape
    cos, su, sd = _rope_tables(pos)
    out = jax.ShapeDtypeStruct((n, d), F32)
    return pl.pallas_call(
        _qkv_sample_kernel,
        out_shape=[out, out, out],
        compiler_params=pltpu.CompilerParams(vmem_limit_bytes=VMEM_LIMIT),
        name="qkv_sample",
    )(x, g.reshape(1, d), w_bf16, cos, su, sd)


SEL_PAGES_PER_STEP = 16
Q_PAD = 8


def _past_gate_kernel(pt_ref, q_ref, *rest):
    page_refs = rest[:SEL_PAGES_PER_STEP]
    sel_ref = rest[SEL_PAGES_PER_STEP]
    km_scr = rest[SEL_PAGES_PER_STEP + 1]
    i = pl.program_id(1)
    n_blk = pl.num_programs(1) * (SEL_PAGES_PER_STEP // PAGES_PER_BLOCK)
    lane3 = lax.broadcasted_iota(jnp.int32, km_scr.shape, 2)

    @pl.when(i == 0)
    def _():
        km_scr[...] = jnp.zeros(km_scr.shape, F32)

    km = km_scr[...]
    for c in range(SEL_PAGES_PER_STEP // PAGES_PER_BLOCK):
        tot = page_refs[c * PAGES_PER_BLOCK][0, 0]
        for p in range(1, PAGES_PER_BLOCK):
            tot = tot + page_refs[c * PAGES_PER_BLOCK + p][0, 0]
        mean = jnp.sum(tot, axis=2, keepdims=True) * (1.0 / MOBA_BLOCK)
        km = jnp.where(lane3 == i * (SEL_PAGES_PER_STEP // PAGES_PER_BLOCK) + c, mean, km)
    km_scr[...] = km

    @pl.when(i == pl.num_programs(1) - 1)
    def _():
        gates = [jnp.dot(q_ref[0, h], km[h], precision=HIGHEST, preferred_element_type=F32)
                 for h in range(N_HEADS)]
        g = jnp.concatenate(gates, axis=0)
        lane = lax.broadcasted_iota(jnp.int32, g.shape, 1)
        g = jnp.where(lane < n_blk, g, -jnp.inf)
        out = jnp.zeros(g.shape, jnp.int32)
        for r in range(MOBA_TOPK):
            m = jnp.max(g, axis=1, keepdims=True)
            idx = jnp.min(jnp.where(g == m, lane, LANES), axis=1, keepdims=True)
            out = jnp.where(lane == r, idx, out)
            g = jnp.where(lane == idx, -jnp.inf, g)
        sel_ref[0] = out


def _past_gate(page_table_flat, q_pad, cache_kt, layer):
    db, h, _, dh = q_pad.shape
    n_pages = page_table_flat.shape[0] // db
    assert n_pages // PAGES_PER_BLOCK <= LANES

    def page_map(r):
        return lambda bi, i, pt: (layer, pt[bi * n_pages + i * SEL_PAGES_PER_STEP + r], 0, 0, 0)

    page_specs = [pl.BlockSpec((1, 1, h, dh, PAGE_SIZE), page_map(r)) for r in range(SEL_PAGES_PER_STEP)]
    grid_spec = pltpu.PrefetchScalarGridSpec(
        num_scalar_prefetch=1,
        grid=(db, n_pages // SEL_PAGES_PER_STEP),
        in_specs=[pl.BlockSpec((1, h, Q_PAD, dh), lambda bi, i, pt: (bi, 0, 0, 0))] + page_specs,
        out_specs=pl.BlockSpec((1, h * Q_PAD, LANES), lambda bi, i, pt: (bi, 0, 0)),
        scratch_shapes=[pltpu.VMEM((h, dh, LANES), F32)],
    )
    return pl.pallas_call(
        _past_gate_kernel,
        grid_spec=grid_spec,
        out_shape=jax.ShapeDtypeStruct((db, h * Q_PAD, LANES), jnp.int32),
        compiler_params=_params("parallel", "arbitrary"),
        name="past_gate",
    )(page_table_flat, q_pad, *([cache_kt] * SEL_PAGES_PER_STEP))


def _moba_sample_kernel(pt_ref, sel_ref, qt_ref, knt_ref, vnt_ref, ck_hbm, cv_hbm, o_ref, kbuf, vbuf, sems,
                        *, layer, t_new, n_pages):
    i = pl.program_id(0)
    n_steps = pl.num_programs(0)

    def page_copies(step, slot):
        bi = step // N_HEADS
        hi = step % N_HEADS
        copies = []
        for t in range(t_new):
            for n in range(MOBA_TOPK):
                blk = sel_ref[((bi * N_HEADS + hi) * Q_PAD + t) * MOBA_TOPK + n]
                for half in range(PAGES_PER_BLOCK):
                    r = (t * MOBA_TOPK + n) * PAGES_PER_BLOCK + half
                    page = pt_ref[bi * n_pages + blk * PAGES_PER_BLOCK + half]
                    copies.append(pltpu.make_async_copy(ck_hbm.at[layer, page, hi], kbuf.at[slot, r], sems.at[0, slot]))
                    copies.append(pltpu.make_async_copy(cv_hbm.at[layer, page, hi], vbuf.at[slot, r], sems.at[1, slot]))
        return copies

    slot = i % 2

    @pl.when(i == 0)
    def _():
        for c in page_copies(i, slot):
            c.start()

    @pl.when(i + 1 < n_steps)
    def _():
        for c in page_copies(i + 1, 1 - slot):
            c.start()

    for c in page_copies(i, slot):
        c.wait()

    qt = qt_ref[0, 0] * HEAD_DIM ** -0.5
    knt = knt_ref[0, 0]
    vnt = vnt_ref[0, 0]
    lane_new = lax.broadcasted_iota(jnp.int32, (1, Q_PAD), 1)
    lane_out = lax.broadcasted_iota(jnp.int32, (HEAD_DIM, Q_PAD), 1)
    out_t = jnp.zeros((HEAD_DIM, Q_PAD), F32)
    pages_per_token = MOBA_TOPK * PAGES_PER_BLOCK
    for t in range(t_new):
        q_col = qt[:, t:t + 1]
        s_new = jnp.sum(knt * q_col, axis=0, keepdims=True)
        s_new = jnp.where(lane_new <= t, s_new, NEG_BIG)
        s_sel = [jnp.sum(kbuf[slot, t * pages_per_token + r] * q_col, axis=0, keepdims=True)
                 for r in range(pages_per_token)]
        m_sel = s_sel[0]
        for s in s_sel[1:]:
            m_sel = jnp.maximum(m_sel, s)
        m = jnp.maximum(jnp.max(m_sel, axis=1, keepdims=True), jnp.max(s_new, axis=1, keepdims=True))
        p_new = jnp.exp(s_new - m)
        l = jnp.sum(p_new, axis=1, keepdims=True)
        acc = None
        p_sum = None
        for r, s in enumerate(s_sel):
            p = jnp.exp(s - m)
            p_sum = p if p_sum is None else p_sum + p
            term = vbuf[slot, t * pages_per_token + r] * p
            acc = term if acc is None else acc + term
        l = l + jnp.sum(p_sum, axis=1, keepdims=True)
        o_col = jnp.sum(acc, axis=1, keepdims=True) + jnp.sum(vnt * p_new, axis=1, keepdims=True)
        out_t = jnp.where(lane_out == t, o_col / l, out_t)
    o_ref[0, 0] = out_t


def _moba_sample(page_table_flat, sel_flat, qt_pad, knt_pad, vnt_pad, cache_kt, cache_vt, layer, t_new):
    db, h, dh, _ = qt_pad.shape
    n_pages = page_table_flat.shape[0] // db
    n_buf = t_new * MOBA_TOPK * PAGES_PER_BLOCK
    new_spec = pl.BlockSpec((1, 1, dh, Q_PAD), lambda i, pt, sel: (i // h, i % h, 0, 0))
    hbm_spec = pl.BlockSpec(memory_space=pl.ANY)
    grid_spec = pltpu.PrefetchScalarGridSpec(
        num_scalar_prefetch=2,
        grid=(db * h,),
        in_specs=[new_spec, new_spec, new_spec, hbm_spec, hbm_spec],
        out_specs=new_spec,
        scratch_shapes=[
            pltpu.VMEM((2, n_buf, dh, PAGE_SIZE), F32),
            pltpu.VMEM((2, n_buf, dh, PAGE_SIZE), F32),
            pltpu.SemaphoreType.DMA((2, 2)),
        ],
    )
    return pl.pallas_call(
        functools.partial(_moba_sample_kernel, layer=layer, t_new=t_new, n_pages=n_pages),
        grid_spec=grid_spec,
        out_shape=jax.ShapeDtypeStruct((db, h, dh, Q_PAD), F32),
        compiler_params=_params("arbitrary"),
        name="moba_sample",
    )(page_table_flat, sel_flat, qt_pad, knt_pad, vnt_pad, cache_kt, cache_vt)


def _pad_cols(x_bhdt):
    db, h, dh, t_new = x_bhdt.shape
    return jnp.concatenate([x_bhdt, jnp.zeros((db, h, dh, Q_PAD - t_new), x_bhdt.dtype)], axis=3)


def _pad_rows(x_bhtd):
    db, h, t_new, dh = x_bhtd.shape
    return jnp.concatenate([x_bhtd, jnp.zeros((db, h, Q_PAD - t_new, dh), x_bhtd.dtype)], axis=2)


PROMPT_ROWS = 512
MOE_PROMPT_ROWS = 1024


def kernel(x_prompt, x_sample, cache_k, cache_v, page_table, state_conv, norm_mix, norm_ffn, norm_final,
           w_qkv, w_o, w_pw1, b_pw1, w_dw, b_dw, conv_ln_g, conv_ln_b, w_pw2, b_pw2,
           w_group_router, b_group_router, w_expert_router, b_expert_router,
           w_exp_gate, w_exp_up, w_exp_down):
    b, s, d = x_prompt.shape
    db, t_new, _ = x_sample.shape
    assert page_table.shape[1] % PAGES_PER_BLOCK == 0
    past_len = page_table.shape[1] * PAGE_SIZE
    n_p = b * s
    n_s = db * t_new

    def moe(h, layer, rows, final):
        w_r, b_r = _router_params(w_group_router[layer], b_group_router[layer],
                                  w_expert_router[layer], b_expert_router[layer])
        return _moe(h, norm_ffn[layer], w_r, b_r, w_exp_gate[layer].astype(BF16), w_exp_up[layer].astype(BF16),
                    w_exp_down[layer].astype(BF16), norm_final, rows, final)

    wqkv = w_qkv[0].astype(BF16)
    wo = w_o[0].astype(BF16)
    qt, k2, vt, kt_paged, vt_paged, kmean = _qkv_prompt(x_prompt, norm_mix[0], wqkv)
    n_blk = s // MOBA_BLOCK
    kmean = kmean.reshape(b, n_blk, N_PAIRS, LANES).transpose(0, 2, 1, 3)
    attn_p = _moba_prompt(qt, k2, vt, kmean)
    hp = _linear_residual(attn_p.reshape(n_p, d), wo, x_prompt.reshape(n_p, d), PROMPT_ROWS)

    xs = x_sample.reshape(n_s, d)
    pos_s = past_len + jnp.tile(jnp.arange(t_new, dtype=jnp.int32), db)
    qs, ks, vs = _qkv_sample(xs, norm_mix[0], wqkv, pos_s)
    q_bthd = qs.reshape(db, t_new, N_HEADS, HEAD_DIM)
    k_bthd = ks.reshape(db, t_new, N_HEADS, HEAD_DIM)
    v_bthd = vs.reshape(db, t_new, N_HEADS, HEAD_DIM)
    k_bhtd = k_bthd.transpose(0, 2, 1, 3)
    v_bhtd = v_bthd.transpose(0, 2, 1, 3)
    pt_flat = page_table.reshape(-1)
    cache_kt = jnp.swapaxes(cache_k, 3, 4)
    cache_vt = jnp.swapaxes(cache_v, 3, 4)
    sel = _past_gate(pt_flat, _pad_rows(q_bthd.transpose(0, 2, 1, 3)), cache_kt, 0)[:, :, :MOBA_TOPK]
    to_t = lambda x: _pad_cols(x.transpose(0, 2, 3, 1))
    attn_s = _moba_sample(pt_flat, sel.reshape(-1), to_t(q_bthd), to_t(k_bthd), to_t(v_bthd),
                          cache_kt, cache_vt, 0, t_new)
    attn_s = attn_s[:, :, :, :t_new].transpose(0, 3, 1, 2).reshape(n_s, d).astype(BF16)
    hs = _linear_residual(attn_s, wo, xs, n_s)

    hp = moe(hp, 0, MOE_PROMPT_ROWS, False)
    hs = moe(hs, 0, n_s, False)

    w1 = w_pw1[0].astype(BF16)
    w2 = w_pw2[0].astype(BF16)
    conv_w = (norm_mix[1], w1, b_pw1[0], w_dw[0], b_dw[0], conv_ln_g[0], conv_ln_b[0], w2, b_pw2[0])
    hp, conv_p = _conv_prompt(hp.reshape(b, s, d), *conv_w)
    hs_tb, u_tb = _conv_sample(hs.reshape(db, t_new, d).transpose(1, 0, 2), state_conv[0].transpose(1, 0, 2), *conv_w)
    hs = hs_tb.transpose(1, 0, 2).reshape(n_s, d)
    conv_s = jnp.concatenate([state_conv[0], u_tb.transpose(1, 0, 2)], axis=1)[:, t_new:]

    y_prompt = moe(hp.reshape(n_p, d), 1, MOE_PROMPT_ROWS, True).reshape(b, s, d)
    y_sample = moe(hs, 1, n_s, True).reshape(db, t_new, d)

    return (y_prompt, y_sample, jnp.swapaxes(kt_paged, 3, 4)[None], jnp.swapaxes(vt_paged, 3, 4)[None],
            k_bhtd[None], v_bhtd[None], conv_p[None], conv_s[None])
```

```python
import functools

import jax
import jax.numpy as jnp
from jax import lax
from jax.experimental import pallas as pl
from jax.experimental.pallas import tpu as pltpu

D_MODEL = 1024
N_HEADS = 16
HEAD_DIM = D_MODEL // N_HEADS
N_PAIRS = N_HEADS // 2
ROT_DIM = HEAD_DIM // 4
ROPE_THETA = 500000.0
MOBA_BLOCK = 256
MOBA_TOPK = 3
PAGE_SIZE = 128
PAGES_PER_BLOCK = MOBA_BLOCK // PAGE_SIZE
CONV_WIDTH = 31
N_GROUPS = 4
EXPERTS_PER_GROUP = 8
N_EXPERTS = N_GROUPS * EXPERTS_PER_GROUP
D_EXPERT = 256
NORM_EPS = 1e-6
LANES = 128
SUBLANES = 8
NEG_BIG = -1e30
VMEM_LIMIT = 56 * 1024 * 1024
F32 = jnp.float32
BF16 = jnp.bfloat16
HIGHEST = lax.Precision.HIGHEST
CONTRACT_LAST = (((1,), (1,)), ((), ()))


def _params(*sem):
    return pltpu.CompilerParams(dimension_semantics=sem, vmem_limit_bytes=VMEM_LIMIT)


def _rms(x, g):
    return x * lax.rsqrt(jnp.mean(x * x, axis=-1, keepdims=True) + NORM_EPS) * g


def _rope(x, cos, sin_up, sin_dn):
    reps = x.shape[1] // LANES
    c = jnp.tile(cos, (1, reps))
    su = jnp.tile(sin_up, (1, reps))
    sd = jnp.tile(sin_dn, (1, reps))
    half = ROT_DIM // 2
    return x * c + pltpu.roll(x, half, 1) * su + pltpu.roll(x, x.shape[1] - half, 1) * sd


def _rope_tables(pos):
    half = ROT_DIM // 2
    inv_freq = jnp.power(jnp.asarray(ROPE_THETA, F32), -jnp.arange(0, ROT_DIM, 2, dtype=F32) / ROT_DIM)
    ang = pos.astype(F32)[:, None] * inv_freq[None, :]
    cos, sin = jnp.cos(ang), jnp.sin(ang)
    rows = pos.shape[0]
    ones = jnp.ones((rows, HEAD_DIM - ROT_DIM), F32)
    zeros = jnp.zeros((rows, HEAD_DIM - ROT_DIM), F32)
    zh = jnp.zeros((rows, half), F32)
    cos_h = jnp.concatenate([cos, cos, ones], axis=1)
    up_h = jnp.concatenate([zh, sin, zeros], axis=1)
    dn_h = jnp.concatenate([-sin, zh, zeros], axis=1)
    reps = LANES // HEAD_DIM
    return jnp.tile(cos_h, (1, reps)), jnp.tile(up_h, (1, reps)), jnp.tile(dn_h, (1, reps))


QKV_ROWS = 256
V_AUG_ROWS = HEAD_DIM + 16


def _qkv_prompt_kernel(x_ref, g_ref, w_ref, cos_ref, su_ref, sd_ref,
                       qt_ref, k2_ref, vt_ref, kp_ref, vp_ref, km_ref):
    xn = _rms(x_ref[0], g_ref[...]).astype(BF16)
    acc = jnp.dot(xn, w_ref[...], preferred_element_type=F32)
    cos, su, sd = cos_ref[...], su_ref[...], sd_ref[...]
    q = _rope(acc[:, :D_MODEL], cos, su, sd)
    k = _rope(acc[:, D_MODEL:2 * D_MODEL], cos, su, sd)
    v = acc[:, 2 * D_MODEL:]
    qt, kt, vt = q.T, k.T, v.T
    qt_ref[0, :, 0] = qt.reshape(N_PAIRS, LANES, QKV_ROWS)
    kth = kt.reshape(N_HEADS, HEAD_DIM, QKV_ROWS)
    vth = vt.reshape(N_HEADS, HEAD_DIM, QKV_ROWS)
    lane = lax.broadcasted_iota(jnp.int32, (QKV_ROWS, LANES), 1)
    block_onehot = jnp.where(lane == HEAD_DIM + pl.program_id(1), 1.0, 0.0)
    low = lane < HEAD_DIM
    for pr in range(N_PAIRS):
        k_pair = k[:, pr * LANES:(pr + 1) * LANES]
        k2_ref[0, 2 * pr, 0] = jnp.where(low, k_pair, block_onehot).astype(BF16)
        k2_ref[0, 2 * pr + 1, 0] = jnp.where(low, pltpu.roll(k_pair, HEAD_DIM, 1), block_onehot).astype(BF16)
    ones = jnp.ones((V_AUG_ROWS - HEAD_DIM, QKV_ROWS), F32)
    for h in range(N_HEADS):
        vt_ref[0, h, 0] = jnp.concatenate([vth[h], ones], axis=0).astype(BF16)
    for p in range(QKV_ROWS // PAGE_SIZE):
        rows = slice(p * PAGE_SIZE, (p + 1) * PAGE_SIZE)
        kp_ref[0, p] = kth[:, :, rows]
        vp_ref[0, p] = vth[:, :, rows]
    km_ref[0, 0] = jnp.sum(k, axis=0, keepdims=True) * (1.0 / MOBA_BLOCK)


def _qkv_prompt(x, g, w_bf16):
    b, s, d = x.shape
    nt = s // QKV_ROWS
    cos, su, sd = _rope_tables(jnp.arange(s, dtype=jnp.int32))
    tab_spec = pl.BlockSpec((QKV_ROWS, LANES), lambda bi, si: (si, 0))
    assert QKV_ROWS == MOBA_BLOCK and HEAD_DIM + nt <= LANES
    head_map = lambda bi, si: (bi, 0, si, 0, 0)
    qt_spec = pl.BlockSpec((1, N_PAIRS, 1, LANES, QKV_ROWS), head_map)
    k_spec = pl.BlockSpec((1, N_HEADS, 1, QKV_ROWS, LANES), head_map)
    vt_spec = pl.BlockSpec((1, N_HEADS, 1, V_AUG_ROWS, QKV_ROWS), head_map)
    page_spec = pl.BlockSpec((1, QKV_ROWS // PAGE_SIZE, N_HEADS, HEAD_DIM, PAGE_SIZE),
                             lambda bi, si: (bi, si, 0, 0, 0))
    return pl.pallas_call(
        _qkv_prompt_kernel,
        grid=(b, nt),
        in_specs=[
            pl.BlockSpec((1, QKV_ROWS, d), lambda bi, si: (bi, si, 0)),
            pl.BlockSpec((1, d), lambda bi, si: (0, 0)),
            pl.BlockSpec((d, 3 * d), lambda bi, si: (0, 0)),
            tab_spec, tab_spec, tab_spec,
        ],
        out_specs=[
            qt_spec, k_spec, vt_spec, page_spec, page_spec,
            pl.BlockSpec((1, 1, 1, d), lambda bi, si: (bi, si, 0, 0)),
        ],
        out_shape=[
            jax.ShapeDtypeStruct((b, N_PAIRS, nt, LANES, QKV_ROWS), F32),
            jax.ShapeDtypeStruct((b, N_HEADS, nt, QKV_ROWS, LANES), BF16),
            jax.ShapeDtypeStruct((b, N_HEADS, nt, V_AUG_ROWS, QKV_ROWS), BF16),
            jax.ShapeDtypeStruct((b, s // PAGE_SIZE, N_HEADS, HEAD_DIM, PAGE_SIZE), F32),
            jax.ShapeDtypeStruct((b, s // PAGE_SIZE, N_HEADS, HEAD_DIM, PAGE_SIZE), F32),
            jax.ShapeDtypeStruct((b, nt, 1, d), F32),
        ],
        compiler_params=_params("parallel", "parallel"),
        name="qkv_prompt",
    )(x, g.reshape(1, d), w_bf16, cos, su, sd)


LOG2_E = 1.4426950408889634
ATTN_GROUPS = (8, 4, 2, 1)


def _topk_block_bias(gate, blk, valid):
    n_blk = gate.shape[0]
    g = jnp.where(valid, gate, -jnp.inf)
    sel = jnp.zeros(gate.shape, jnp.bool_)
    for _ in range(MOBA_TOPK):
        m = jnp.max(g, axis=0, keepdims=True)
        hit = jnp.logical_and(g == m, m > -jnp.inf)
        idx = jnp.min(jnp.where(hit, blk, float(n_blk)), axis=0, keepdims=True)
        pick = blk == idx
        sel = jnp.logical_or(sel, pick)
        g = jnp.where(pick, -jnp.inf, g)
    return jnp.where(sel, 0.0, NEG_BIG)


def _moba_prompt_kernel(qt_ref, k_ref, vt_ref, km_ref, o_ref):
    qi = pl.program_id(2)
    tq = qt_ref.shape[4]
    n_blk = km_ref.shape[2]
    qt = qt_ref[0, 0, 0]
    km = km_ref[0, 0]
    is_a = lax.broadcasted_iota(jnp.int32, (1, LANES), 1) < HEAD_DIM
    km_heads = jnp.concatenate([jnp.where(is_a, km, 0.0), jnp.where(is_a, 0.0, km)], axis=0)
    gates_ab = jnp.dot(km_heads, qt, precision=HIGHEST, preferred_element_type=F32)
    gates = (gates_ab[:n_blk], gates_ab[n_blk:])
    blk_id = lax.broadcasted_iota(jnp.int32, (n_blk, tq), 0)
    valid = blk_id < qi
    blk = blk_id.astype(F32)
    scale = HEAD_DIM ** -0.5 * LOG2_E
    pad = jnp.zeros((LANES - HEAD_DIM - n_blk, tq), F32)
    q_own, q_aug = [], []
    for hh in range(2):
        bias = _topk_block_bias(gates[hh], blk, valid)
        q_h = qt[hh * HEAD_DIM:(hh + 1) * HEAD_DIM] * scale
        q_own.append(jnp.concatenate([q_h, jnp.zeros((LANES - HEAD_DIM, tq), F32)], axis=0).astype(BF16))
        q_aug.append(jnp.concatenate([q_h, bias, pad], axis=0).astype(BF16))

    def head_scores(hh, blocks, q_side=q_aug):
        return [jnp.dot(k_ref[0, hh, j], q_side[hh], preferred_element_type=F32) for j in blocks]

    def head_update(s, hh, j, m, acc):
        m_new = jnp.maximum(m, jnp.max(s, axis=0, keepdims=True))
        p = jnp.exp2(s - m_new).astype(BF16)
        acc = jnp.exp2(m - m_new) * acc + jnp.dot(vt_ref[0, hh, j], p, preferred_element_type=F32)
        return m_new, acc

    def absorb(blocks, sa, sb, state):
        ma, acc_a, mb, acc_b = state
        for j, sja, sjb in zip(blocks, sa, sb):
            ma, acc_a = head_update(sja, 0, j, ma, acc_a)
            mb, acc_b = head_update(sjb, 1, j, mb, acc_b)
        return ma, acc_a, mb, acc_b

    def attend(blocks, mask, state):
        if mask is None:
            sa, sb = head_scores(0, blocks), head_scores(1, blocks)
        else:
            sa = [s + mask for s in head_scores(0, blocks, q_own)]
            sb = [s + mask for s in head_scores(1, blocks, q_own)]
        return absorb(blocks, sa, sb, state)

    key = lax.broadcasted_iota(jnp.int32, (tq, tq), 0)
    qry = lax.broadcasted_iota(jnp.int32, (tq, tq), 1)
    causal = jnp.where(key <= qry, 0.0, NEG_BIG)
    m0 = jnp.full((1, tq), NEG_BIG, F32)
    acc0 = jnp.zeros((V_AUG_ROWS, tq), F32)
    state = attend([qi], causal, (m0, acc0, m0, acc0))

    done = 0
    for width in ATTN_GROUPS:
        n_groups = (qi - done) // width
        state = lax.fori_loop(
            0, n_groups, lambda g, st, w=width, d=done: attend([d + g * w + u for u in range(w)], None, st), state)
        done = done + n_groups * width
    _, acc_a, _, acc_b = state
    out_t = jnp.concatenate([acc_a[:HEAD_DIM] / acc_a[HEAD_DIM:HEAD_DIM + 1],
                             acc_b[:HEAD_DIM] / acc_b[HEAD_DIM:HEAD_DIM + 1]], axis=0)
    o_ref[0] = out_t.T.astype(o_ref.dtype)


def _moba_prompt(qt, k_aug, vt_aug, kmean):
    b, n_pairs, n_blk, _, tq = qt.shape
    kv_map = lambda bi, pi, qi: (bi, pi, 0, 0, 0)
    return pl.pallas_call(
        _moba_prompt_kernel,
        grid=(b, n_pairs, n_blk),
        in_specs=[
            pl.BlockSpec((1, 1, 1, LANES, tq), lambda bi, pi, qi: (bi, pi, qi, 0, 0)),
            pl.BlockSpec((1, 2, n_blk, tq, LANES), kv_map),
            pl.BlockSpec((1, 2, n_blk, V_AUG_ROWS, tq), kv_map),
            pl.BlockSpec((1, 1, n_blk, LANES), lambda bi, pi, qi: (bi, pi, 0, 0)),
        ],
        out_specs=pl.BlockSpec((1, tq, LANES), lambda bi, pi, qi: (bi, qi, pi)),
        out_shape=jax.ShapeDtypeStruct((b, n_blk * tq, n_pairs * LANES), BF16),
        compiler_params=_params("parallel", "parallel", "parallel"),
        name="moba_prompt",
    )(qt, k_aug, vt_aug, kmean)


def _linear_residual_kernel(x_ref, w_ref, r_ref, o_ref):
    o_ref[...] = r_ref[...] + jnp.dot(x_ref[...], w_ref[...], preferred_element_type=F32)


def _linear_residual(x_bf16, w_bf16, resid, rows):
    n, k = x_bf16.shape
    m = w_bf16.shape[1]
    return pl.pallas_call(
        _linear_residual_kernel,
        grid=(n // rows,),
        in_specs=[
            pl.BlockSpec((rows, k), lambda i: (i, 0)),
            pl.BlockSpec((k, m), lambda i: (0, 0)),
            pl.BlockSpec((rows, m), lambda i: (i, 0)),
        ],
        out_specs=pl.BlockSpec((rows, m), lambda i: (i, 0)),
        out_shape=jax.ShapeDtypeStruct((n, m), F32),
        compiler_params=_params("parallel"),
        name="linear_residual",
    )(x_bf16, w_bf16, resid)


MOE_EXPERTS_PER_STEP = 4
ROUTER_GROUP_LANE0 = N_EXPERTS
EXPERT_GROUP_SHIFT = EXPERTS_PER_GROUP.bit_length() - 1
assert 1 << EXPERT_GROUP_SHIFT == EXPERTS_PER_GROUP


def _route(logits):
    lane = lax.broadcasted_iota(jnp.int32, logits.shape, 1)
    is_g = jnp.logical_and(lane >= ROUTER_GROUP_LANE0, lane < ROUTER_GROUP_LANE0 + N_GROUPS)
    lg = jnp.where(is_g, logits, -jnp.inf)
    mg = jnp.max(lg, axis=1, keepdims=True)
    zg = jnp.sum(jnp.exp(lg - mg), axis=1, keepdims=True)
    g_w = 1.0 / zg
    g_idx = jnp.min(jnp.where(lg == mg, lane - ROUTER_GROUP_LANE0, N_GROUPS), axis=1, keepdims=True)
    in_grp = jnp.logical_and(lane < N_EXPERTS, jnp.right_shift(lane, EXPERT_GROUP_SHIFT) == g_idx)
    le = jnp.where(in_grp, logits, -jnp.inf)
    m1 = jnp.max(le, axis=1, keepdims=True)
    ze = jnp.sum(jnp.exp(le - m1), axis=1, keepdims=True)
    idx1 = jnp.min(jnp.where(le == m1, lane, LANES), axis=1, keepdims=True)
    le2 = jnp.where(lane == idx1, -jnp.inf, le)
    m2 = jnp.max(le2, axis=1, keepdims=True)
    idx2 = jnp.min(jnp.where(le2 == m2, lane, LANES), axis=1, keepdims=True)
    w1 = 1.0 / ze
    w2 = jnp.exp(m2 - m1) / ze
    c1 = g_w * w1 / (w1 + w2)
    c2 = g_w * w2 / (w1 + w2)
    return jnp.where(lane == idx1, c1, 0.0) + jnp.where(lane == idx2, c2, 0.0)


def _moe_kernel(h_ref, g_ref, wr_ref, br_ref, wg_ref, wu_ref, wd_ref, gf_ref, o_ref,
                xn_scr, gate_scr, acc_scr, *, final_norm):
    e = pl.program_id(1)

    @pl.when(e == 0)
    def _():
        h = h_ref[...]
        xn = _rms(h, g_ref[...])
        xh = xn.astype(BF16)
        xn_scr[...] = xh
        xl = (xn - xh.astype(F32)).astype(BF16)
        w = wr_ref[...]
        wh = w.astype(BF16)
        wl = (w - wh.astype(F32)).astype(BF16)
        logits = (jnp.dot(xh, wh, preferred_element_type=F32) + jnp.dot(xh, wl, preferred_element_type=F32)
                  + jnp.dot(xl, wh, preferred_element_type=F32)) + br_ref[...]
        gate_scr[...] = _route(logits)
        acc_scr[...] = h

    xn = xn_scr[...]
    gates = gate_scr[...]
    lane = lax.broadcasted_iota(jnp.int32, gates.shape, 1)
    acc = acc_scr[...]
    for c in range(MOE_EXPERTS_PER_STEP):
        eid = e * MOE_EXPERTS_PER_STEP + c
        gcol = jnp.sum(jnp.where(lane == eid, gates, 0.0), axis=1, keepdims=True)
        hg = jnp.dot(xn, wg_ref[c], preferred_element_type=F32)
        hu = jnp.dot(xn, wu_ref[c], preferred_element_type=F32)
        hdn = (hg * jax.nn.sigmoid(hg)) * hu * gcol
        acc = acc + jnp.dot(hdn.astype(BF16), wd_ref[c], preferred_element_type=F32)
    acc_scr[...] = acc

    @pl.when(e == pl.num_programs(1) - 1)
    def _():
        if final_norm:
            o_ref[...] = _rms(acc, gf_ref[...])
        else:
            o_ref[...] = acc


def _moe(h, g, w_router, b_router, wg, wu, wd, g_final, rows, final_norm):
    n, d = h.shape
    steps = N_EXPERTS // MOE_EXPERTS_PER_STEP
    row_spec = pl.BlockSpec((rows, d), lambda i, e: (i, 0))
    vec_spec = pl.BlockSpec((1, d), lambda i, e: (0, 0))
    return pl.pallas_call(
        functools.partial(_moe_kernel, final_norm=final_norm),
        grid=(n // rows, steps),
        in_specs=[
            row_spec, vec_spec,
            pl.BlockSpec((d, LANES), lambda i, e: (0, 0)),
            pl.BlockSpec((1, LANES), lambda i, e: (0, 0)),
            pl.BlockSpec((MOE_EXPERTS_PER_STEP, d, D_EXPERT), lambda i, e: (e, 0, 0)),
            pl.BlockSpec((MOE_EXPERTS_PER_STEP, d, D_EXPERT), lambda i, e: (e, 0, 0)),
            pl.BlockSpec((MOE_EXPERTS_PER_STEP, D_EXPERT, d), lambda i, e: (e, 0, 0)),
            vec_spec,
        ],
        out_specs=row_spec,
        out_shape=jax.ShapeDtypeStruct((n, d), F32),
        scratch_shapes=[
            pltpu.VMEM((rows, d), BF16),
            pltpu.VMEM((rows, LANES), F32),
            pltpu.VMEM((rows, d), F32),
        ],
        compiler_params=_params("parallel", "arbitrary"),
        name="hier_moe",
    )(h, g.reshape(1, d), w_router, b_router, wg, wu, wd, g_final.reshape(1, d))


def _router_params(w_gr, b_gr, w_er, b_er):
    d = w_gr.shape[0]
    pad = LANES - N_EXPERTS - N_GROUPS
    w = jnp.concatenate([w_er, w_gr, jnp.zeros((d, pad), F32)], axis=1)
    b = jnp.concatenate([b_er, b_gr, jnp.zeros((pad,), F32)]).reshape(1, LANES)
    return w, b


CONV_ROWS = 512
CONV_HALO = 32
CONV_ROW_CHUNK = 64
CONV_LANE_CHUNK = 256


def _layer_norm_silu(y, g, b):
    mu = jnp.mean(y, axis=-1, keepdims=True)
    yc = y - mu
    var = jnp.mean(yc * yc, axis=-1, keepdims=True)
    z = yc * lax.rsqrt(var + NORM_EPS) * g + b
    return z * jax.nn.sigmoid(z)


def _conv_prompt_kernel(x_ref, g_ref, w1_ref, b1_ref, wdw_ref, bdw_ref, lng_ref, lnb_ref, w2_ref, b2_ref,
                        o_ref, cb_ref, u_scr, y_scr, shift_scr):
    si = pl.program_id(1)
    rows = x_ref.shape[1]

    @pl.when(si == 0)
    def _():
        u_scr[0:CONV_HALO, :] = jnp.zeros((CONV_HALO, D_MODEL), F32)

    @pl.when(si > 0)
    def _():
        u_scr[0:CONV_HALO, :] = u_scr[rows:rows + CONV_HALO, :]

    x = x_ref[0]
    xn = _rms(x, g_ref[...]).astype(BF16)
    a = jnp.dot(xn, w1_ref[...], preferred_element_type=F32) + b1_ref[...]
    u_scr[CONV_HALO:CONV_HALO + rows, :] = a[:, :D_MODEL] * jax.nn.sigmoid(a[:, D_MODEL:])

    lead = CONV_HALO - (CONV_WIDTH - 1)
    span = shift_scr.shape[1]
    for sh in range(1, SUBLANES):
        shift_scr[sh - 1] = u_scr[sh:sh + span, :]
    for r0 in range(0, rows, CONV_ROW_CHUNK):
        for c0 in range(0, D_MODEL, CONV_LANE_CHUNK):
            cols = slice(c0, c0 + CONV_LANE_CHUNK)
            y = jnp.broadcast_to(bdw_ref[:, cols], (CONV_ROW_CHUNK, CONV_LANE_CHUNK))
            for w in range(CONV_WIDTH):
                base, sh = divmod(lead + w, SUBLANES)
                start = r0 + base * SUBLANES
                if sh == 0:
                    win = u_scr[start:start + CONV_ROW_CHUNK, cols]
                else:
                    win = shift_scr[sh - 1, start:start + CONV_ROW_CHUNK, cols]
                y = y + win * wdw_ref[w:w + 1, cols]
            y_scr[r0:r0 + CONV_ROW_CHUNK, cols] = y

    z = _layer_norm_silu(y_scr[...], lng_ref[...], lnb_ref[...])
    o_ref[0] = x + jnp.dot(z.astype(BF16), w2_ref[...], preferred_element_type=F32) + b2_ref[...]

    @pl.when(si == pl.num_programs(1) - 1)
    def _():
        cb_ref[0] = u_scr[rows + lead:rows + CONV_HALO, :]


def _conv_prompt(x, g, w1, b1, wdw, bdw, lng, lnb, w2, b2):
    b, s, d = x.shape
    vec = lambda n: pl.BlockSpec((1, n), lambda bi, si: (0, 0))
    return pl.pallas_call(
        _conv_prompt_kernel,
        grid=(b, s // CONV_ROWS),
        in_specs=[
            pl.BlockSpec((1, CONV_ROWS, d), lambda bi, si: (bi, si, 0)),
            vec(d),
            pl.BlockSpec((d, 2 * d), lambda bi, si: (0, 0)),
            vec(2 * d),
            pl.BlockSpec((CONV_WIDTH, d), lambda bi, si: (0, 0)),
            vec(d), vec(d), vec(d),
            pl.BlockSpec((d, d), lambda bi, si: (0, 0)),
            vec(d),
        ],
        out_specs=[
            pl.BlockSpec((1, CONV_ROWS, d), lambda bi, si: (bi, si, 0)),
            pl.BlockSpec((1, CONV_WIDTH - 1, d), lambda bi, si: (bi, 0, 0)),
        ],
        out_shape=[
            jax.ShapeDtypeStruct((b, s, d), F32),
            jax.ShapeDtypeStruct((b, CONV_WIDTH - 1, d), F32),
        ],
        scratch_shapes=[
            pltpu.VMEM((CONV_HALO + CONV_ROWS, d), F32),
            pltpu.VMEM((CONV_ROWS, d), F32),
            pltpu.VMEM((SUBLANES - 1, CONV_HALO + CONV_ROWS - SUBLANES, d), F32),
        ],
        compiler_params=_params("parallel", "arbitrary"),
        name="conformer_conv_prompt",
    )(x, g.reshape(1, d), w1, b1.reshape(1, 2 * d), wdw, bdw.reshape(1, d), lng.reshape(1, d),
      lnb.reshape(1, d), w2, b2.reshape(1, d))


def _conv_sample_kernel(x_ref, st_ref, g_ref, w1_ref, b1_ref, wdw_ref, bdw_ref, lng_ref, lnb_ref, w2_ref, b2_ref,
                        o_ref, u_ref):
    t_new, db, d = x_ref.shape
    n_state = st_ref.shape[0]
    for t in range(t_new):
        xn = _rms(x_ref[t], g_ref[...]).astype(BF16)
        a = jnp.dot(xn, w1_ref[...], preferred_element_type=F32) + b1_ref[...]
        u_ref[t] = a[:, :D_MODEL] * jax.nn.sigmoid(a[:, D_MODEL:])
    for t in range(t_new):
        y = jnp.broadcast_to(bdw_ref[...], (db, d))
        for w in range(CONV_WIDTH):
            r = t + w
            row = st_ref[r] if r < n_state else u_ref[r - n_state]
            y = y + row * wdw_ref[w:w + 1, :]
        z = _layer_norm_silu(y, lng_ref[...], lnb_ref[...])
        o_ref[t] = x_ref[t] + jnp.dot(z.astype(BF16), w2_ref[...], preferred_element_type=F32) + b2_ref[...]


def _conv_sample(x_tb, state_tb, g, w1, b1, wdw, bdw, lng, lnb, w2, b2):
    t_new, db, d = x_tb.shape
    return pl.pallas_call(
        _conv_sample_kernel,
        out_shape=[
            jax.ShapeDtypeStruct((t_new, db, d), F32),
            jax.ShapeDtypeStruct((t_new, db, d), F32),
        ],
        compiler_params=pltpu.CompilerParams(vmem_limit_bytes=VMEM_LIMIT),
        name="conformer_conv_sample",
    )(x_tb, state_tb, g.reshape(1, d), w1, b1.reshape(1, 2 * d), wdw, bdw.reshape(1, d), lng.reshape(1, d),
      lnb.reshape(1, d), w2, b2.reshape(1, d))


def _qkv_sample_kernel(x_ref, g_ref, w_ref, cos_ref, su_ref, sd_ref, q_ref, k_ref, v_ref):
    xn = _rms(x_ref[...], g_ref[...]).astype(BF16)
    acc = jnp.dot(xn, w_ref[...], preferred_element_type=F32)
    cos, su, sd = cos_ref[...], su_ref[...], sd_ref[...]
    q_ref[...] = _rope(acc[:, :D_MODEL], cos, su, sd)
    k_ref[...] = _rope(acc[:, D_MODEL:2 * D_MODEL], cos, su, sd)
    v_ref[...] = acc[:, 2 * D_MODEL:]


def _qkv_sample(x, g, w_bf16, pos):
    n, d = x.shape
    cos, su, sd = _rope_tables(pos)
    out = jax.ShapeDtypeStruct((n, d), F32)
    return pl.pallas_call(
        _qkv_sample_kernel,
        out_shape=[out, out, out],
        compiler_params=pltpu.CompilerParams(vmem_limit_bytes=VMEM_LIMIT),
        name="qkv_sample",
    )(x, g.reshape(1, d), w_bf16, cos, su, sd)


SEL_PAGES_PER_STEP = 16
Q_PAD = 8


def _past_gate_kernel(pt_ref, q_ref, *rest):
    page_refs = rest[:SEL_PAGES_PER_STEP]
    sel_ref = rest[SEL_PAGES_PER_STEP]
    km_scr = rest[SEL_PAGES_PER_STEP + 1]
    i = pl.program_id(1)
    n_blk = pl.num_programs(1) * (SEL_PAGES_PER_STEP // PAGES_PER_BLOCK)
    lane3 = lax.broadcasted_iota(jnp.int32, km_scr.shape, 2)

    @pl.when(i == 0)
    def _():
        km_scr[...] = jnp.zeros(km_scr.shape, F32)

    km = km_scr[...]
    for c in range(SEL_PAGES_PER_STEP // PAGES_PER_BLOCK):
        tot = page_refs[c * PAGES_PER_BLOCK][0, 0]
        for p in range(1, PAGES_PER_BLOCK):
            tot = tot + page_refs[c * PAGES_PER_BLOCK + p][0, 0]
        mean = jnp.sum(tot, axis=2, keepdims=True) * (1.0 / MOBA_BLOCK)
        km = jnp.where(lane3 == i * (SEL_PAGES_PER_STEP // PAGES_PER_BLOCK) + c, mean, km)
    km_scr[...] = km

    @pl.when(i == pl.num_programs(1) - 1)
    def _():
        gates = [jnp.dot(q_ref[0, h], km[h], precision=HIGHEST, preferred_element_type=F32)
                 for h in range(N_HEADS)]
        g = jnp.concatenate(gates, axis=0)
        lane = lax.broadcasted_iota(jnp.int32, g.shape, 1)
        g = jnp.where(lane < n_blk, g, -jnp.inf)
        out = jnp.zeros(g.shape, jnp.int32)
        for r in range(MOBA_TOPK):
            m = jnp.max(g, axis=1, keepdims=True)
            idx = jnp.min(jnp.where(g == m, lane, LANES), axis=1, keepdims=True)
            out = jnp.where(lane == r, idx, out)
            g = jnp.where(lane == idx, -jnp.inf, g)
        sel_ref[0] = out


def _past_gate(page_table_flat, q_pad, cache_kt, layer):
    db, h, _, dh = q_pad.shape
    n_pages = page_table_flat.shape[0] // db
    assert n_pages // PAGES_PER_BLOCK <= LANES

    def page_map(r):
        return lambda bi, i, pt: (layer, pt[bi * n_pages + i * SEL_PAGES_PER_STEP + r], 0, 0, 0)

    page_specs = [pl.BlockSpec((1, 1, h, dh, PAGE_SIZE), page_map(r)) for r in range(SEL_PAGES_PER_STEP)]
    grid_spec = pltpu.PrefetchScalarGridSpec(
        num_scalar_prefetch=1,
        grid=(db, n_pages // SEL_PAGES_PER_STEP),
        in_specs=[pl.BlockSpec((1, h, Q_PAD, dh), lambda bi, i, pt: (bi, 0, 0, 0))] + page_specs,
        out_specs=pl.BlockSpec((1, h * Q_PAD, LANES), lambda bi, i, pt: (bi, 0, 0)),
        scratch_shapes=[pltpu.VMEM((h, dh, LANES), F32)],
    )
    return pl.pallas_call(
        _past_gate_kernel,
        grid_spec=grid_spec,
        out_shape=jax.ShapeDtypeStruct((db, h * Q_PAD, LANES), jnp.int32),
        compiler_params=_params("parallel", "arbitrary"),
        name="past_gate",
    )(page_table_flat, q_pad, *([cache_kt] * SEL_PAGES_PER_STEP))


def _moba_sample_kernel(pt_ref, sel_ref, qt_ref, knt_ref, vnt_ref, ck_hbm, cv_hbm, o_ref, kbuf, vbuf, sems,
                        *, layer, t_new, n_pages):
    i = pl.program_id(0)
    n_steps = pl.num_programs(0)

    def page_copies(step, slot):
        bi = step // N_HEADS
        hi = step % N_HEADS
        copies = []
        for t in range(t_new):
            for n in range(MOBA_TOPK):
                blk = sel_ref[((bi * N_HEADS + hi) * Q_PAD + t) * MOBA_TOPK + n]
                for half in range(PAGES_PER_BLOCK):
                    r = (t * MOBA_TOPK + n) * PAGES_PER_BLOCK + half
                    page = pt_ref[bi * n_pages + blk * PAGES_PER_BLOCK + half]
                    copies.append(pltpu.make_async_copy(ck_hbm.at[layer, page, hi], kbuf.at[slot, r], sems.at[0, slot]))
                    copies.append(pltpu.make_async_copy(cv_hbm.at[layer, page, hi], vbuf.at[slot, r], sems.at[1, slot]))
        return copies

    slot = i % 2

    @pl.when(i == 0)
    def _():
        for c in page_copies(i, slot):
            c.start()

    @pl.when(i + 1 < n_steps)
    def _():
        for c in page_copies(i + 1, 1 - slot):
            c.start()

    for c in page_copies(i, slot):
        c.wait()

    qt = qt_ref[0, 0] * HEAD_DIM ** -0.5
    knt = knt_ref[0, 0]
    vnt = vnt_ref[0, 0]
    lane_new = lax.broadcasted_iota(jnp.int32, (1, Q_PAD), 1)
    lane_out = lax.broadcasted_iota(jnp.int32, (HEAD_DIM, Q_PAD), 1)
    out_t = jnp.zeros((HEAD_DIM, Q_PAD), F32)
    pages_per_token = MOBA_TOPK * PAGES_PER_BLOCK
    for t in range(t_new):
        q_col = qt[:, t:t + 1]
        s_new = jnp.sum(knt * q_col, axis=0, keepdims=True)
        s_new = jnp.where(lane_new <= t, s_new, NEG_BIG)
        s_sel = [jnp.sum(kbuf[slot, t * pages_per_token + r] * q_col, axis=0, keepdims=True)
                 for r in range(pages_per_token)]
        m_sel = s_sel[0]
        for s in s_sel[1:]:
            m_sel = jnp.maximum(m_sel, s)
        m = jnp.maximum(jnp.max(m_sel, axis=1, keepdims=True), jnp.max(s_new, axis=1, keepdims=True))
        p_new = jnp.exp(s_new - m)
        l = jnp.sum(p_new, axis=1, keepdims=True)
        acc = None
        p_sum = None
        for r, s in enumerate(s_sel):
            p = jnp.exp(s - m)
            p_sum = p if p_sum is None else p_sum + p
            term = vbuf[slot, t * pages_per_token + r] * p
            acc = term if acc is None else acc + term
        l = l + jnp.sum(p_sum, axis=1, keepdims=True)
        o_col = jnp.sum(acc, axis=1, keepdims=True) + jnp.sum(vnt * p_new, axis=1, keepdims=True)
        out_t = jnp.where(lane_out == t, o_col / l, out_t)
    o_ref[0, 0] = out_t


def _moba_sample(page_table_flat, sel_flat, qt_pad, knt_pad, vnt_pad, cache_kt, cache_vt, layer, t_new):
    db, h, dh, _ = qt_pad.shape
    n_pages = page_table_flat.shape[0] // db
    n_buf = t_new * MOBA_TOPK * PAGES_PER_BLOCK
    new_spec = pl.BlockSpec((1, 1, dh, Q_PAD), lambda i, pt, sel: (i // h, i % h, 0, 0))
    hbm_spec = pl.BlockSpec(memory_space=pl.ANY)
    grid_spec = pltpu.PrefetchScalarGridSpec(
        num_scalar_prefetch=2,
        grid=(db * h,),
        in_specs=[new_spec, new_spec, new_spec, hbm_spec, hbm_spec],
        out_specs=new_spec,
        scratch_shapes=[
            pltpu.VMEM((2, n_buf, dh, PAGE_SIZE), F32),
            pltpu.VMEM((2, n_buf, dh, PAGE_SIZE), F32),
            pltpu.SemaphoreType.DMA((2, 2)),
        ],
    )
    return pl.pallas_call(
        functools.partial(_moba_sample_kernel, layer=layer, t_new=t_new, n_pages=n_pages),
        grid_spec=grid_spec,
        out_shape=jax.ShapeDtypeStruct((db, h, dh, Q_PAD), F32),
        compiler_params=_params("arbitrary"),
        name="moba_sample",
    )(page_table_flat, sel_flat, qt_pad, knt_pad, vnt_pad, cache_kt, cache_vt)


def _pad_cols(x_bhdt):
    db, h, dh, t_new = x_bhdt.shape
    return jnp.concatenate([x_bhdt, jnp.zeros((db, h, dh, Q_PAD - t_new), x_bhdt.dtype)], axis=3)


def _pad_rows(x_bhtd):
    db, h, t_new, dh = x_bhtd.shape
    return jnp.concatenate([x_bhtd, jnp.zeros((db, h, Q_PAD - t_new, dh), x_bhtd.dtype)], axis=2)


PROMPT_ROWS = 512
MOE_PROMPT_ROWS = 1024


def kernel(x_prompt, x_sample, cache_k, cache_v, page_table, state_conv, norm_mix, norm_ffn, norm_final,
           w_qkv, w_o, w_pw1, b_pw1, w_dw, b_dw, conv_ln_g, conv_ln_b, w_pw2, b_pw2,
           w_group_router, b_group_router, w_expert_router, b_expert_router,
           w_exp_gate, w_exp_up, w_exp_down):
    b, s, d = x_prompt.shape
    db, t_new, _ = x_sample.shape
    assert page_table.shape[1] % PAGES_PER_BLOCK == 0
    past_len = page_table.shape[1] * PAGE_SIZE
    n_p = b * s
    n_s = db * t_new

    def moe(h, layer, rows, final):
        w_r, b_r = _router_params(w_group_router[layer], b_group_router[layer],
                                  w_expert_router[layer], b_expert_router[layer])
        return _moe(h, norm_ffn[layer], w_r, b_r, w_exp_gate[layer].astype(BF16), w_exp_up[layer].astype(BF16),
                    w_exp_down[layer].astype(BF16), norm_final, rows, final)

    wqkv = w_qkv[0].astype(BF16)
    wo = w_o[0].astype(BF16)
    qt, k2, vt, kt_paged, vt_paged, kmean = _qkv_prompt(x_prompt, norm_mix[0], wqkv)
    n_blk = s // MOBA_BLOCK
    kmean = kmean.reshape(b, n_blk, N_PAIRS, LANES).transpose(0, 2, 1, 3)
    attn_p = _moba_prompt(qt, k2, vt, kmean)
    hp = _linear_residual(attn_p.reshape(n_p, d), wo, x_prompt.reshape(n_p, d), PROMPT_ROWS)

    xs = x_sample.reshape(n_s, d)
    pos_s = past_len + jnp.tile(jnp.arange(t_new, dtype=jnp.int32), db)
    qs, ks, vs = _qkv_sample(xs, norm_mix[0], wqkv, pos_s)
    q_bthd = qs.reshape(db, t_new, N_HEADS, HEAD_DIM)
    k_bthd = ks.reshape(db, t_new, N_HEADS, HEAD_DIM)
    v_bthd = vs.reshape(db, t_new, N_HEADS, HEAD_DIM)
    k_bhtd = k_bthd.transpose(0, 2, 1, 3)
    v_bhtd = v_bthd.transpose(0, 2, 1, 3)
    pt_flat = page_table.reshape(-1)
    cache_kt = jnp.swapaxes(cache_k, 3, 4)
    cache_vt = jnp.swapaxes(cache_v, 3, 4)
    sel = _past_gate(pt_flat, _pad_rows(q_bthd.transpose(0, 2, 1, 3)), cache_kt, 0)[:, :, :MOBA_TOPK]
    to_t = lambda x: _pad_cols(x.transpose(0, 2, 3, 1))
    attn_s = _moba_sample(pt_flat, sel.reshape(-1), to_t(q_bthd), to_t(k_bthd), to_t(v_bthd),
                          cache_kt, cache_vt, 0, t_new)
    attn_s = attn_s[:, :, :, :t_new].transpose(0, 3, 1, 2).reshape(n_s, d).astype(BF16)
    hs = _linear_residual(attn_s, wo, xs, n_s)

    hp = moe(hp, 0, MOE_PROMPT_ROWS, False)
    hs = moe(hs, 0, n_s, False)

    w1 = w_pw1[0].astype(BF16)
    w2 = w_pw2[0].astype(BF16)
    conv_w = (norm_mix[1], w1, b_pw1[0], w_dw[0], b_dw[0], conv_ln_g[0], conv_ln_b[0], w2, b_pw2[0])
    hp, conv_p = _conv_prompt(hp.reshape(b, s, d), *conv_w)
    hs_tb, u_tb = _conv_sample(hs.reshape(db, t_new, d).transpose(1, 0, 2), state_conv[0].transpose(1, 0, 2), *conv_w)
    hs = hs_tb.transpose(1, 0, 2).reshape(n_s, d)
    conv_s = jnp.concatenate([state_conv[0], u_tb.transpose(1, 0, 2)], axis=1)[:, t_new:]

    y_prompt = moe(hp.reshape(n_p, d), 1, MOE_PROMPT_ROWS, True).reshape(b, s, d)
    y_sample = moe(hs, 1, n_s, True).reshape(db, t_new, d)

    return (y_prompt, y_sample, jnp.swapaxes(kt_paged, 3, 4)[None], jnp.swapaxes(vt_paged, 3, 4)[None],
            k_bhtd[None], v_bhtd[None], conv_p[None], conv_s[None])
```
